```python
import math
import jax, jax.numpy as jnp
from jax import lax
import numpy as np

D_MODEL = 2048
BATCH = 8
SEQ = 4096
DEPTH = 4
DEC_BATCH = 1
DEC_SEQ = 8192
PAST_LEN = 128

MIX_WIDTH = D_MODEL
S5_WIDTH = MIX_WIDTH // 4
POOL_WIDTH = MIX_WIDTH // 4
RET_WIDTH = MIX_WIDTH // 2
S5_GROUP = 16
S5_GROUPS = S5_WIDTH // S5_GROUP
S5_STATE = 64
POOL_WINDOWS = (2, 4, 8, 16)
POOL_GROUP = POOL_WIDTH // len(POOL_WINDOWS)
RET_HEADS = 4
RET_HEAD_DIM = RET_WIDTH // RET_HEADS
RET_CHUNK = 128
IN_WIDTH = S5_WIDTH + POOL_WIDTH + 4 * RET_WIDTH
PEER_HEADS = 8
PEER_QDIM = 256
PEER_HALF = PEER_QDIM // 2
PEER_NKEYS = 128
PEER_EXPERTS = PEER_NKEYS * PEER_NKEYS
PEER_TOPK = 16
PEER_TOKEN_BLOCK = 128
ALPHA = (2 * DEPTH) ** 0.25
BETA = (8 * DEPTH) ** -0.25
LN_EPS = 1e-5
GN_EPS = 1e-6

kernel_name = 'hybrid_s5_pool_retention_peer_encoder'

F32 = jnp.float32


def layer_norm(x, g, b):
    xf = x.astype(F32)
    mu = jnp.mean(xf, axis=-1, keepdims=True)
    var = jnp.mean(jnp.square(xf - mu), axis=-1, keepdims=True)
    y = (xf - mu) * lax.rsqrt(var + LN_EPS) * g.astype(F32) + b.astype(F32)
    return y.astype(x.dtype)


def _cmul(ar, ai, br, bi):
    return ar * br - ai * bi, ar * bi + ai * br


def _lin_rec_combine(e1, e2):
    a1r, a1i, b1r, b1i = e1
    a2r, a2i, b2r, b2i = e2
    ar, ai = _cmul(a1r, a1i, a2r, a2i)
    br, bi = _cmul(a2r, a2i, b1r, b1i)
    return ar, ai, br + b2r, bi + b2i


def s5_direction(u, lam_re, lam_im, log_step, b_re, b_im, c_re, c_im, reverse):
    lr = lam_re.astype(F32)
    li = lam_im.astype(F32)
    dt = jnp.exp(log_step.astype(F32))[:, None]
    mag = jnp.exp(lr * dt)
    abar_r = mag * jnp.cos(li * dt)
    abar_i = mag * jnp.sin(li * dt)
    den = lr * lr + li * li
    zr = ((abar_r - 1.0) * lr + abar_i * li) / den
    zi = (abar_i * lr - (abar_r - 1.0) * li) / den
    bb_r, bb_i = _cmul(zr[..., None], zi[..., None], b_re.astype(F32), b_im.astype(F32))
    bu_r = jnp.einsum('gpc,blgc->blgp', bb_r, u)
    bu_i = jnp.einsum('gpc,blgc->blgp', bb_i, u)
    a_r = jnp.broadcast_to(abar_r, bu_r.shape)
    a_i = jnp.broadcast_to(abar_i, bu_i.shape)
    _, _, xr, xi = lax.associative_scan(_lin_rec_combine, (a_r, a_i, bu_r, bu_i), reverse=reverse, axis=1)
    return jnp.einsum('gcp,blgp->blgc', c_re.astype(F32), xr) - jnp.einsum('gcp,blgp->blgc', c_im.astype(F32), xi)


def s5_mixer(u, lam_re, lam_im, log_step, b_re, b_im, c_re, c_im, d, w_glu, b_glu):
    bsz, L, _ = u.shape
    uf = u.astype(F32)
    ug = uf.reshape(bsz, L, S5_GROUPS, S5_GROUP)
    y = (s5_direction(ug, lam_re[0], lam_im[0], log_step[0], b_re[0], b_im[0], c_re[0], c_im[0], False)
         + s5_direction(ug, lam_re[1], lam_im[1], log_step[1], b_re[1], b_im[1], c_re[1], c_im[1], True))
    y = y.reshape(bsz, L, S5_WIDTH) + d.astype(F32) * uf
    y = jax.nn.gelu(y, approximate=False)
    return y * jax.nn.sigmoid(y @ w_glu.astype(F32) + b_glu.astype(F32))


def pool_mixer(u, pool_w, pool_scale):
    bsz, L, _ = u.shape
    uf = u.astype(F32)
    cs = jnp.concatenate([jnp.zeros((bsz, 1, POOL_WIDTH), F32), jnp.cumsum(uf, axis=1)], axis=1)
    t = np.arange(L)
    outs = []
    for gi, w in enumerate(POOL_WINDOWS):
        lo = np.maximum(t - w // 2, 0)
        hi = np.minimum(t + w // 2 - 1, L - 1)
        cnt = (hi - lo + 1).astype(np.float32)
        sl = slice(gi * POOL_GROUP, (gi + 1) * POOL_GROUP)
        csg = cs[:, :, sl]
        mean = (csg[:, hi + 1] - csg[:, lo]) / cnt[None, :, None]
        outs.append(mean - uf[:, :, sl])
    p = jnp.stack(outs, axis=2)
    p = jnp.einsum('blgc,gcd->blgd', p, pool_w.astype(F32)).reshape(bsz, L, POOL_WIDTH)
    return p * pool_scale.astype(F32)


def rotary(x):
    L = x.shape[1]
    half = RET_HEAD_DIM // 2
    inv = 1.0 / (10000.0 ** jnp.linspace(0.0, 1.0, half, dtype=F32))
    ang = jnp.arange(L, dtype=F32)[:, None] * inv[None, :]
    cos = jnp.cos(ang)[:, None, :]
    sin = jnp.sin(ang)[:, None, :]
    x1, x2 = x[..., :half], x[..., half:]
    return jnp.concatenate([x1 * cos - x2 * sin, x2 * cos + x1 * sin], axis=-1)


def retention_mixer(q, k, v, g):
    bsz, L, _ = q.shape
    nc = L // RET_CHUNK
    B = RET_CHUNK
    q = rotary(q.astype(F32).reshape(bsz, L, RET_HEADS, RET_HEAD_DIM))
    k = rotary(k.astype(F32).reshape(bsz, L, RET_HEADS, RET_HEAD_DIM)) * (RET_HEAD_DIM ** -0.5)
    v = v.astype(F32).reshape(bsz, L, RET_HEADS, RET_HEAD_DIM)

    def to_chunks(t):
        return t.reshape(bsz, nc, B, RET_HEADS, RET_HEAD_DIM).transpose(1, 0, 3, 2, 4)

    qc, kc, vc = to_chunks(q), to_chunks(k), to_chunks(v)
    log_gamma = jnp.log(1.0 - 2.0 ** (-5.0 - jnp.arange(RET_HEADS, dtype=F32)))
    pos = jnp.arange(B, dtype=F32)
    lg = log_gamma[:, None]
    decay_intra = jnp.exp(log_gamma[:, None, None] * jnp.abs(pos[:, None] - pos[None, :]))
    scores = jnp.einsum('cbhid,cbhjd->cbhij', qc, kc) * decay_intra
    o = jnp.einsum('cbhij,cbhje->cbhie', scores, vc)
    q_fwd = jnp.exp(lg * (pos + 1.0))[:, :, None]
    k_fwd = jnp.exp(lg * (B - 1.0 - pos))[:, :, None]
    q_bwd = jnp.exp(lg * (B - pos))[:, :, None]
    k_bwd = jnp.exp(lg * pos)[:, :, None]
    chunk_decay = jnp.exp(log_gamma * B)[:, None, None]

    def make_step(qd, kd):
        def step(state, inp):
            qi, ki, vi = inp
            out = jnp.einsum('bhid,bhde->bhie', qi * qd, state)
            state = chunk_decay * state + jnp.einsum('bhjd,bhje->bhde', ki * kd, vi)
            return state, out
        return step

    s0 = jnp.zeros((bsz, RET_HEADS, RET_HEAD_DIM, RET_HEAD_DIM), F32)
    _, o_f = lax.scan(make_step(q_fwd, k_fwd), s0, (qc, kc, vc))
    _, o_b = lax.scan(make_step(q_bwd, k_bwd), s0, (qc, kc, vc), reverse=True)
    o = (o + o_f + o_b).transpose(1, 0, 3, 2, 4).reshape(bsz, L, RET_HEADS, RET_HEAD_DIM)
    mu = jnp.mean(o, axis=-1, keepdims=True)
    var = jnp.mean(jnp.square(o - mu), axis=-1, keepdims=True)
    o = ((o - mu) * lax.rsqrt(var + GN_EPS)).reshape(bsz, L, RET_WIDTH)
    return jax.nn.silu(g.astype(F32)) * o


def peer_ffn(x, w_q, sub_keys, u_tab, v_tab):
    bsz, L, D = x.shape
    q = (x @ w_q).astype(F32).reshape(bsz, L, PEER_HEADS, 2, PEER_HALF)
    s = jnp.einsum('blhpc,pnc->blhpn', q, sub_keys.astype(F32))
    top_s, top_i = lax.top_k(s, PEER_TOPK)
    cand = top_s[..., 0, :, None] + top_s[..., 1, None, :]
    cand_idx = top_i[..., 0, :, None] * PEER_NKEYS + top_i[..., 1, None, :]
    cand = cand.reshape(bsz, L, PEER_HEADS, PEER_TOPK * PEER_TOPK)
    cand_idx = cand_idx.reshape(bsz, L, PEER_HEADS, PEER_TOPK * PEER_TOPK)
    best_s, best_pos = lax.top_k(cand, PEER_TOPK)
    idx = jnp.take_along_axis(cand_idx, best_pos, axis=-1)
    gates = jax.nn.softmax(best_s, axis=-1)
    n_tok = bsz * L
    nb = n_tok // PEER_TOKEN_BLOCK
    xb = x.reshape(nb, PEER_TOKEN_BLOCK, D)
    ib = idx.reshape(nb, PEER_TOKEN_BLOCK, PEER_HEADS * PEER_TOPK)
    gb = gates.reshape(nb, PEER_TOKEN_BLOCK, PEER_HEADS * PEER_TOPK).astype(x.dtype)

    def block(args):
        xt, it, gt = args
        u = jnp.take(u_tab, it, axis=0)
        hdn = jax.nn.gelu(jnp.einsum('td,tkd->tk', xt, u), approximate=False)
        vv = jnp.take(v_tab, it, axis=0)
        return jnp.einsum('tk,tkd->td', gt * hdn, vv)

    out = lax.map(block, (xb, ib, gb))
    return out.reshape(bsz, L, D)


def encoder_layer(x, p, l):
    h = x @ p['w_in'][l]
    r0 = S5_WIDTH + POOL_WIDTH
    u_a = h[..., :S5_WIDTH]
    u_b = h[..., S5_WIDTH:r0]
    q = h[..., r0:r0 + RET_WIDTH]
    k = h[..., r0 + RET_WIDTH:r0 + 2 * RET_WIDTH]
    v = h[..., r0 + 2 * RET_WIDTH:r0 + 3 * RET_WIDTH]
    g = h[..., r0 + 3 * RET_WIDTH:r0 + 4 * RET_WIDTH]
    y_a = s5_mixer(u_a, p['s5_lambda_re'][l], p['s5_lambda_im'][l], p['s5_log_step'][l],
                   p['s5_b_re'][l], p['s5_b_im'][l], p['s5_c_re'][l], p['s5_c_im'][l],
                   p['s5_d'][l], p['s5_w_glu'][l], p['s5_b_glu'][l])
    y_b = pool_mixer(u_b, p['pool_w'][l], p['pool_scale'][l])
    y_c = retention_mixer(q, k, v, g)
    mix = jnp.concatenate([y_a.astype(x.dtype), y_b.astype(x.dtype), y_c.astype(x.dtype)], axis=-1) @ p['w_out'][l]
    x = layer_norm(ALPHA * x + mix, p['ln1_g'][l], p['ln1_b'][l])
    ff = peer_ffn(x, p['peer_w_q'][l], p['peer_sub_keys'][l], p['peer_u'][l], p['peer_v'][l])
    x = layer_norm(ALPHA * x + ff.astype(x.dtype), p['ln2_g'][l], p['ln2_b'][l])
    return x


def trunk(x, p):
    x = layer_norm(x, p['ln_in_g'], p['ln_in_b'])
    for l in range(DEPTH):
        x = encoder_layer(x, p, l)
    return x


def setup_inputs(seed: int = 0) -> dict:
    key = jax.random.key(seed)
    ks = jax.random.split(key, 32)
    n = jax.random.normal
    G, P, C = S5_GROUPS, S5_STATE, S5_GROUP
    lam_im_base = math.pi * jnp.arange(P, dtype=F32)
    return {
        'x_prompt': n(ks[0], (BATCH, SEQ, D_MODEL), F32),
        'x_sample': n(ks[1], (DEC_BATCH, DEC_SEQ, D_MODEL), F32),
        'ln_in_g': 1.0 + 0.02 * n(ks[2], (D_MODEL,), F32),
        'ln_in_b': 0.02 * n(ks[3], (D_MODEL,), F32),
        'w_in': n(ks[4], (DEPTH, D_MODEL, IN_WIDTH), F32) * D_MODEL ** -0.5,
        's5_lambda_re': -0.5 + 0.01 * n(ks[5], (DEPTH, 2, G, P), F32),
        's5_lambda_im': lam_im_base + 0.01 * n(ks[6], (DEPTH, 2, G, P), F32),
        's5_log_step': jax.random.uniform(ks[7], (DEPTH, 2, G), F32, math.log(1e-3), math.log(1e-1)),
        's5_b_re': n(ks[8], (DEPTH, 2, G, P, C), F32) * (2 * C) ** -0.5,
        's5_b_im': n(ks[9], (DEPTH, 2, G, P, C), F32) * (2 * C) ** -0.5,
        's5_c_re': n(ks[10], (DEPTH, 2, G, C, P), F32) * P ** -0.5,
        's5_c_im': n(ks[11], (DEPTH, 2, G, C, P), F32) * P ** -0.5,
        's5_d': n(ks[12], (DEPTH, S5_WIDTH), F32),
        's5_w_glu': n(ks[13], (DEPTH, S5_WIDTH, S5_WIDTH), F32) * S5_WIDTH ** -0.5,
        's5_b_glu': 0.01 * n(ks[14], (DEPTH, S5_WIDTH), F32),
        'pool_w': n(ks[15], (DEPTH, len(POOL_WINDOWS), POOL_GROUP, POOL_GROUP), F32) * POOL_GROUP ** -0.5,
        'pool_scale': 1.0 + 0.02 * n(ks[16], (DEPTH, POOL_WIDTH), F32),
        'w_out': n(ks[17], (DEPTH, MIX_WIDTH, D_MODEL), F32) * (MIX_WIDTH ** -0.5 * BETA),
        'ln1_g': 1.0 + 0.02 * n(ks[18], (DEPTH, D_MODEL), F32),
        'ln1_b': 0.02 * n(ks[19], (DEPTH, D_MODEL), F32),
        'peer_w_q': n(ks[20], (DEPTH, D_MODEL, PEER_HEADS * PEER_QDIM), F32) * D_MODEL ** -0.5,
        'peer_sub_keys': n(ks[21], (DEPTH, 2, PEER_NKEYS, PEER_HALF), F32) * PEER_HALF ** -0.5,
        'peer_u': n(ks[22], (DEPTH, PEER_EXPERTS, D_MODEL), F32) * D_MODEL ** -0.5,
        'peer_v': n(ks[23], (DEPTH, PEER_EXPERTS, D_MODEL), F32) * (BETA * PEER_HEADS ** -0.5),
        'ln2_g': 1.0 + 0.02 * n(ks[24], (DEPTH, D_MODEL), F32),
        'ln2_b': 0.02 * n(ks[25], (DEPTH, D_MODEL), F32),
    }


def reference(x_prompt, x_sample, ln_in_g, ln_in_b, w_in, s5_lambda_re, s5_lambda_im, s5_log_step,
              s5_b_re, s5_b_im, s5_c_re, s5_c_im, s5_d, s5_w_glu, s5_b_glu, pool_w, pool_scale,
              w_out, ln1_g, ln1_b, peer_w_q, peer_sub_keys, peer_u, peer_v, ln2_g, ln2_b):
    p = dict(ln_in_g=ln_in_g, ln_in_b=ln_in_b, w_in=w_in, s5_lambda_re=s5_lambda_re,
             s5_lambda_im=s5_lambda_im, s5_log_step=s5_log_step, s5_b_re=s5_b_re, s5_b_im=s5_b_im,
             s5_c_re=s5_c_re, s5_c_im=s5_c_im, s5_d=s5_d, s5_w_glu=s5_w_glu, s5_b_glu=s5_b_glu,
             pool_w=pool_w, pool_scale=pool_scale, w_out=w_out, ln1_g=ln1_g, ln1_b=ln1_b,
             peer_w_q=peer_w_q, peer_sub_keys=peer_sub_keys, peer_u=peer_u, peer_v=peer_v,
             ln2_g=ln2_g, ln2_b=ln2_b)
    y_prompt = trunk(x_prompt, p)
    y_sample = trunk(x_sample, p)
    return (y_prompt, y_sample)
```

```python
import functools
import math

import jax
import jax.numpy as jnp
from jax import lax
from jax.experimental import pallas as pl
from jax.experimental.pallas import tpu as pltpu

F32 = jnp.float32
BF16 = jnp.bfloat16

D_MODEL = 2048
DEPTH = 4
S5_WIDTH = 512
POOL_WIDTH = 512
RET_WIDTH = 1024
S5_GROUP = 16
S5_GROUPS = 32
S5_STATE = 64
S5_NSTATE = S5_GROUPS * S5_STATE
POOL_WINDOWS = (2, 4, 8, 16)
POOL_GROUP = 128
RET_HEADS = 4
RET_HEAD_DIM = 256
RET_CHUNK = 128
PEER_HEADS = 8
PEER_HALF = 128
PEER_NKEYS = 128
PEER_EXPERTS = PEER_NKEYS * PEER_NKEYS
PEER_TOPK = 16
ALPHA = (2 * DEPTH) ** 0.25
LN_EPS = 1e-5
GN_EPS = 1e-6

LANES = 128
SUBLANES = 8
VMEM_LIMIT = 56 * 1024 * 1024


def _params(*sem):
    return pltpu.CompilerParams(dimension_semantics=sem, vmem_limit_bytes=VMEM_LIMIT)


def _layer_norm(y, g, b):
    mu = jnp.mean(y, axis=-1, keepdims=True)
    yc = y - mu
    var = jnp.mean(yc * yc, axis=-1, keepdims=True)
    return yc * lax.rsqrt(var + LN_EPS) * g + b


def _gelu(x):
    return 0.5 * x * (1.0 + lax.erf(x * (1.0 / math.sqrt(2.0))))


def _ln_kernel(x_ref, g_ref, b_ref, y_ref, yb_ref):
    y = _layer_norm(x_ref[...], g_ref[...], b_ref[...])
    y_ref[...] = y
    yb_ref[...] = y.astype(BF16)


def layer_norm_rows(x, g, b, tm=256):
    m, d = x.shape
    return pl.pallas_call(
        _ln_kernel,
        grid=(m // tm,),
        in_specs=[pl.BlockSpec((tm, d), lambda i: (i, 0)),
                  pl.BlockSpec((1, d), lambda i: (0, 0)),
                  pl.BlockSpec((1, d), lambda i: (0, 0))],
        out_specs=[pl.BlockSpec((tm, d), lambda i: (i, 0)),
                   pl.BlockSpec((tm, d), lambda i: (i, 0))],
        out_shape=[jax.ShapeDtypeStruct((m, d), F32), jax.ShapeDtypeStruct((m, d), BF16)],
        compiler_params=_params("parallel"),
        name="ln_in",
    )(x, g.reshape(1, d), b.reshape(1, d))


def _mm_kernel(x_ref, w_ref, o_ref):
    o_ref[...] = jnp.dot(x_ref[...], w_ref[...], preferred_element_type=F32).astype(o_ref.dtype)


def matmul(xb, w, out_dtype, tm, tn, time_major_batch=None, name="mm"):
    m, k = xb.shape
    n = w.shape[1]
    if time_major_batch is None:
        out_shape = jax.ShapeDtypeStruct((m, n), out_dtype)
        out_spec = pl.BlockSpec((tm, tn), lambda i, j: (i, j))
    else:
        bsz, seq = time_major_batch
        assert n == tn and seq % tm == 0 and bsz * seq == m
        tiles = seq // tm
        out_shape = jax.ShapeDtypeStruct((seq, bsz * n), out_dtype)
        out_spec = pl.BlockSpec((tm, tn), lambda i, j: (i % tiles, i // tiles))
    return pl.pallas_call(
        _mm_kernel,
        grid=(m // tm, n // tn),
        in_specs=[pl.BlockSpec((tm, k), lambda i, j: (i, 0)),
                  pl.BlockSpec((k, tn), lambda i, j: (0, j))],
        out_specs=out_spec,
        out_shape=out_shape,
        compiler_params=_params("parallel", "arbitrary"),
        name=name,
    )(xb, w)


S5_ROWS = 512
S5_COLS = 512


def _s5_kernel(*refs, bsz, reverse, finalize):
    if finalize:
        (u_ref, ar_ref, ai_ref, bbig_ref, cbig_ref, yprev_ref, d_ref, wglu_ref, bglu_ref,
         o_ref, xr_ref, xi_ref, bu_ref) = refs
    else:
        u_ref, ar_ref, ai_ref, bbig_ref, cbig_ref, o_ref, xr_ref, xi_ref, bu_ref = refs

    @pl.when(pl.program_id(0) == 0)
    def _():
        xr_ref[...] = jnp.zeros_like(xr_ref)
        xi_ref[...] = jnp.zeros_like(xi_ref)

    u = u_ref[...]
    bu_ref[...] = jnp.dot(u.astype(BF16), bbig_ref[...], preferred_element_type=F32)
    steps = S5_ROWS // bsz
    for cb in range(S5_NSTATE // S5_COLS):
        cr = pl.ds(cb * S5_COLS, S5_COLS)
        ci = pl.ds(S5_NSTATE + cb * S5_COLS, S5_COLS)
        ar = jnp.broadcast_to(ar_ref[:, cr], (bsz, S5_COLS))
        ai = jnp.broadcast_to(ai_ref[:, cr], (bsz, S5_COLS))

        def step(s, carry, cr=cr, ci=ci, ar=ar, ai=ai):
            xr, xi = carry
            t = (steps - 1 - s) if reverse else s
            rows = pl.ds(pl.multiple_of(t * bsz, bsz), bsz)
            nr = ar * xr - ai * xi + bu_ref[rows, cr]
            ni = ar * xi + ai * xr + bu_ref[rows, ci]
            bu_ref[rows, cr] = nr
            bu_ref[rows, ci] = ni
            return nr, ni

        xr, xi = lax.fori_loop(0, steps, step, (xr_ref[:, cr], xi_ref[:, cr]), unroll=8)
        xr_ref[:, cr] = xr
        xi_ref[:, cr] = xi
    y = jnp.dot(bu_ref[...].astype(BF16), cbig_ref[...], preferred_element_type=F32)
    if finalize:
        y = _gelu(y + yprev_ref[...] + d_ref[...] * u)
        z = jnp.dot(y.astype(BF16), wglu_ref[...], preferred_element_type=F32) + bglu_ref[...]
        o_ref[...] = (y * jax.nn.sigmoid(z)).astype(o_ref.dtype)
    else:
        o_ref[...] = y


def s5_direction(u, abar_r, abar_i, bbig, cbig, bsz, reverse, fin=None):
    m = u.shape[0]
    nc = m // S5_ROWS
    cidx = (lambda c: (nc - 1 - c, 0)) if reverse else (lambda c: (c, 0))
    const = lambda c: (0, 0)
    in_specs = [pl.BlockSpec((S5_ROWS, S5_WIDTH), cidx),
                pl.BlockSpec((1, S5_NSTATE), const),
                pl.BlockSpec((1, S5_NSTATE), const),
                pl.BlockSpec((S5_WIDTH, 2 * S5_NSTATE), const),
                pl.BlockSpec((2 * S5_NSTATE, S5_WIDTH), const)]
    args = [u, abar_r, abar_i, bbig, cbig]
    if fin is not None:
        y_prev, d, w_glu, b_glu = fin
        in_specs += [pl.BlockSpec((S5_ROWS, S5_WIDTH), cidx),
                     pl.BlockSpec((1, S5_WIDTH), const),
                     pl.BlockSpec((S5_WIDTH, S5_WIDTH), const),
                     pl.BlockSpec((1, S5_WIDTH), const)]
        args += [y_prev, d, w_glu, b_glu]
    return pl.pallas_call(
        functools.partial(_s5_kernel, bsz=bsz, reverse=reverse, finalize=fin is not None),
        grid=(nc,),
        in_specs=in_specs,
        out_specs=pl.BlockSpec((S5_ROWS, S5_WIDTH), cidx),
        out_shape=jax.ShapeDtypeStruct((m, S5_WIDTH), BF16 if fin is not None else F32),
        scratch_shapes=[pltpu.VMEM((bsz, S5_NSTATE), F32),
                        pltpu.VMEM((bsz, S5_NSTATE), F32),
                        pltpu.VMEM((S5_ROWS, 2 * S5_NSTATE), F32)],
        compiler_params=_params("arbitrary"),
        name="s5_bwd" if reverse else "s5_fwd",
    )(*args)


def s5_weights(lam_re, lam_im, log_step, b_re, b_im, c_re, c_im):
    dt = jnp.exp(log_step)[:, None]
    mag = jnp.exp(lam_re * dt)
    abar_r = mag * jnp.cos(lam_im * dt)
    abar_i = mag * jnp.sin(lam_im * dt)
    den = lam_re * lam_re + lam_im * lam_im
    zr = ((abar_r - 1.0) * lam_re + abar_i * lam_im) / den
    zi = (abar_i * lam_re - (abar_r - 1.0) * lam_im) / den
    bb_r = zr[..., None] * b_re - zi[..., None] * b_im
    bb_i = zr[..., None] * b_im + zi[..., None] * b_re
    eye = jnp.eye(S5_GROUPS, dtype=F32)

    def b_block(bb):
        return jnp.einsum('gpc,gh->gchp', bb, eye).reshape(S5_WIDTH, S5_NSTATE)

    def c_block(cc):
        return jnp.einsum('gcp,gh->gphc', cc, eye).reshape(S5_NSTATE, S5_WIDTH)

    bbig = jnp.concatenate([b_block(bb_r), b_block(bb_i)], axis=1).astype(BF16)
    cbig = jnp.concatenate([c_block(c_re), -c_block(c_im)], axis=0).astype(BF16)
    return abar_r.reshape(1, S5_NSTATE), abar_i.reshape(1, S5_NSTATE), bbig, cbig


POOL_TILE = 256
POOL_HALO = 64


def _pool_kernel(prev_ref, u_ref, next_ref, w_ref, scale_ref, o_ref, *, tiles_per_seq, seq_len):
    t0 = (pl.program_id(0) % tiles_per_seq) * POOL_TILE
    u = u_ref[...]
    ext = jnp.concatenate([prev_ref[...], u, next_ref[...]], axis=0)
    n_ext = POOL_TILE + 2 * POOL_HALO
    t_out = t0 + lax.broadcasted_iota(jnp.int32, (POOL_TILE, 1), 0)
    t_src = t0 - POOL_HALO + lax.broadcasted_iota(jnp.int32, (1, n_ext), 1)
    outs = []
    for gi, w in enumerate(POOL_WINDOWS):
        lo = jnp.maximum(t_out - w // 2, 0)
        hi = jnp.minimum(t_out + w // 2 - 1, seq_len - 1)
        band = jnp.where((t_src >= lo) & (t_src <= hi), 1.0, 0.0).astype(F32)
        cols = slice(gi * POOL_GROUP, (gi + 1) * POOL_GROUP)
        tot = jnp.dot(band, ext[:, cols], preferred_element_type=F32, precision=lax.Precision.HIGHEST)
        p = tot / (hi - lo + 1).astype(F32) - u[:, cols]
        outs.append(jnp.dot(p.astype(BF16), w_ref[gi], preferred_element_type=F32))
    o_ref[...] = (jnp.concatenate(outs, axis=1) * scale_ref[...]).astype(o_ref.dtype)


def pool_mixer(u, w, scale, seq_len):
    m = u.shape[0]
    tiles_per_seq = seq_len // POOL_TILE
    r = POOL_TILE // POOL_HALO
    last = m // POOL_HALO - 1
    return pl.pallas_call(
        functools.partial(_pool_kernel, tiles_per_seq=tiles_per_seq, seq_len=seq_len),
        grid=(m // POOL_TILE,),
        in_specs=[pl.BlockSpec((POOL_HALO, POOL_WIDTH), lambda i: (jnp.maximum(i * r - 1, 0), 0)),
                  pl.BlockSpec((POOL_TILE, POOL_WIDTH), lambda i: (i, 0)),
                  pl.BlockSpec((POOL_HALO, POOL_WIDTH), lambda i: (jnp.minimum((i + 1) * r, last), 0)),
                  pl.BlockSpec((len(POOL_WINDOWS), POOL_GROUP, POOL_GROUP), lambda i: (0, 0, 0)),
                  pl.BlockSpec((1, POOL_WIDTH), lambda i: (0, 0))],
        out_specs=pl.BlockSpec((POOL_TILE, POOL_WIDTH), lambda i: (i, 0)),
        out_shape=jax.ShapeDtypeStruct((m, POOL_WIDTH), BF16),
        compiler_params=_params("parallel"),
        name="pool",
    )(u, u, u, w, scale)


def _ret_kernel(*refs, reverse, finalize):
    if finalize:
        q_ref, k_ref, v_ref, cos_ref, sin_ref, g_ref, part_ref, o_ref, state_ref = refs
    else:
        q_ref, k_ref, v_ref, cos_ref, sin_ref, o_ref, state_ref = refs

    @pl.when(pl.program_id(2) == 0)
    def _():
        state_ref[...] = jnp.zeros_like(state_ref)

    cos = cos_ref[...]
    sin = sin_ref[...]
    half = RET_HEAD_DIM // 2

    def rotary(x):
        x1, x2 = x[:, :half], x[:, half:]
        return jnp.concatenate([x1 * cos - x2 * sin, x2 * cos + x1 * sin], axis=1)

    q = rotary(q_ref[...])
    k = rotary(k_ref[...]) * (RET_HEAD_DIM ** -0.5)
    vb = v_ref[...].astype(BF16)
    n = RET_CHUNK
    head = jnp.full((n, 1), pl.program_id(1), jnp.int32).astype(F32)
    log_gamma = jnp.log(1.0 - jnp.exp2(-5.0 - head))
    pos = lax.broadcasted_iota(jnp.int32, (n, 1), 0).astype(F32)
    if reverse:
        q_decay = jnp.exp(log_gamma * (n - pos))
        k_decay = jnp.exp(log_gamma * pos)
    else:
        q_decay = jnp.exp(log_gamma * (pos + 1.0))
        k_decay = jnp.exp(log_gamma * (n - 1.0 - pos))
    chunk_decay = jnp.exp(log_gamma[0:1, :] * n)
    state = state_ref[...]
    o = jnp.dot((q * q_decay).astype(BF16), state.astype(BF16), preferred_element_type=F32)
    if not reverse:
        dist = jnp.abs(lax.broadcasted_iota(jnp.int32, (n, n), 0)
                       - lax.broadcasted_iota(jnp.int32, (n, n), 1)).astype(F32)
        scores = lax.dot_general(q.astype(BF16), k.astype(BF16), (((1,), (1,)), ((), ())),
                                 preferred_element_type=F32) * jnp.exp(log_gamma * dist)
        o = o + jnp.dot(scores.astype(BF16), vb, preferred_element_type=F32)
    kv = lax.dot_general((k * k_decay).astype(BF16), vb, (((0,), (0,)), ((), ())),
                         preferred_element_type=F32)
    state_ref[...] = chunk_decay * state + kv
    if finalize:
        o = o + part_ref[...]
        mu = jnp.mean(o, axis=-1, keepdims=True)
        oc = o - mu
        var = jnp.mean(oc * oc, axis=-1, keepdims=True)
        g = g_ref[...]
        o_ref[...] = (g * jax.nn.sigmoid(g) * (oc * lax.rsqrt(var + GN_EPS))).astype(o_ref.dtype)
    else:
        o_ref[...] = o


def retention_direction(hq, cos, sin, bsz, seq, reverse, part=None):
    m = hq.shape[0]
    nc = seq // RET_CHUNK
    d = RET_HEAD_DIM

    def rows(b, c):
        return b * nc + ((nc - 1 - c) if reverse else c)

    def col(j):
        return pl.BlockSpec((RET_CHUNK, d), lambda b, h, c, j=j: (rows(b, c), j * RET_HEADS + h))

    tab = pl.BlockSpec((RET_CHUNK, d // 2), lambda b, h, c: ((nc - 1 - c) if reverse else c, 0))
    in_specs = [col(0), col(1), col(2), tab, tab]
    args = [hq, hq, hq, cos, sin]
    if part is not None:
        in_specs += [col(3), pl.BlockSpec((RET_CHUNK, d), lambda b, h, c: (rows(b, c), h))]
        args += [hq, part]
    return pl.pallas_call(
        functools.partial(_ret_kernel, reverse=reverse, finalize=part is not None),
        grid=(bsz, RET_HEADS, nc),
        in_specs=in_specs,
        out_specs=pl.BlockSpec((RET_CHUNK, d), lambda b, h, c: (rows(b, c), h)),
        out_shape=jax.ShapeDtypeStruct((m, RET_WIDTH), BF16 if part is not None else F32),
        scratch_shapes=[pltpu.VMEM((d, d), F32)],
        compiler_params=_params("parallel", "parallel", "arbitrary"),
        name="ret_bwd" if reverse else "ret_fwd",
    )(*args)


def rotary_tables(seq):
    half = RET_HEAD_DIM // 2
    inv = 1.0 / (10000.0 ** jnp.linspace(0.0, 1.0, half, dtype=F32))
    ang = jnp.arange(seq, dtype=F32)[:, None] * inv[None, :]
    return jnp.cos(ang), jnp.sin(ang)


MIX_TILE = 256


def _mix_kernel(x_ref, ya_ref, yb_ref, yc_ref, wa_ref, wb_ref, wc_ref, g_ref, b_ref, o_ref, ob_ref):
    mix = (jnp.dot(ya_ref[...], wa_ref[...], preferred_element_type=F32)
           + jnp.dot(yb_ref[...], wb_ref[...], preferred_element_type=F32)
           + jnp.dot(yc_ref[...], wc_ref[...], preferred_element_type=F32))
    y = _layer_norm(ALPHA * x_ref[...] + mix, g_ref[...], b_ref[...])
    o_ref[...] = y
    ob_ref[...] = y.astype(BF16)


def mix_ln(x, ya_tb, yb, yc, w_out, g, b, bsz, seq):
    m, d = x.shape
    tm = MIX_TILE
    tiles = seq // tm
    row = lambda i: (i, 0)
    const = lambda i: (0, 0)
    return pl.pallas_call(
        _mix_kernel,
        grid=(m // tm,),
        in_specs=[pl.BlockSpec((tm, d), row),
                  pl.BlockSpec((tm, S5_WIDTH), lambda i: (i % tiles, i // tiles)),
                  pl.BlockSpec((tm, POOL_WIDTH), row),
                  pl.BlockSpec((tm, RET_WIDTH), row),
                  pl.BlockSpec((S5_WIDTH, d), lambda i: (0, 0)),
                  pl.BlockSpec((POOL_WIDTH, d), lambda i: (1, 0)),
                  pl.BlockSpec((RET_WIDTH, d), lambda i: (1, 0)),
                  pl.BlockSpec((1, d), const),
                  pl.BlockSpec((1, d), const)],
        out_specs=[pl.BlockSpec((tm, d), row), pl.BlockSpec((tm, d), row)],
        out_shape=[jax.ShapeDtypeStruct((m, d), F32), jax.ShapeDtypeStruct((m, d), BF16)],
        compiler_params=_params("parallel"),
        name="mix_ln",
    )(x, ya_tb, yb, yc, w_out, w_out, w_out, g, b)


def _resid_ln_kernel(x_ref, f_ref, g_ref, b_ref, o_ref, ob_ref):
    y = _layer_norm(ALPHA * x_ref[...] + f_ref[...], g_ref[...], b_ref[...])
    o_ref[...] = y
    ob_ref[...] = y.astype(BF16)


def resid_ln(x, f, g, b, tm=256):
    m, d = x.shape
    row = lambda i: (i, 0)
    const = lambda i: (0, 0)
    return pl.pallas_call(
        _resid_ln_kernel,
        grid=(m // tm,),
        in_specs=[pl.BlockSpec((tm, d), row), pl.BlockSpec((tm, d), row),
                  pl.BlockSpec((1, d), const), pl.BlockSpec((1, d), const)],
        out_specs=[pl.BlockSpec((tm, d), row), pl.BlockSpec((tm, d), row)],
        out_shape=[jax.ShapeDtypeStruct((m, d), F32), jax.ShapeDtypeStruct((m, d), BF16)],
        compiler_params=_params("parallel"),
        name="resid_ln",
    )(x, f, g, b)


PEER_SCORE_TILE = 256
NEG_BIG = -3.0e38
PEER_NCAND = PEER_TOPK + SUBLANES * (PEER_TOPK - 1)


def _write_top_desc(s, n, out_ref):
    for r in range(n):
        m = jnp.max(s, axis=0, keepdims=True)
        out_ref[r:r + 1, :] = m
        if r < n - 1:
            s = jnp.where(s >= m, NEG_BIG, s)


def _peer_scores_kernel(xb_ref, wq_ref, keys_ref, s1_ref, s2_ref, tz_ref, a_scr, b_scr, cand_scr):
    q = jnp.dot(xb_ref[...], wq_ref[...], preferred_element_type=F32)
    for h in range(PEER_HEADS):
        for p, (s_ref, top_scr) in enumerate(((s1_ref, a_scr), (s2_ref, b_scr))):
            c0 = (2 * h + p) * PEER_HALF
            s = lax.dot_general(keys_ref[p], q[:, c0:c0 + PEER_HALF].astype(BF16), (((1,), (1,)), ((), ())),
                                preferred_element_type=F32)
            s = s - jnp.max(s, axis=0, keepdims=True)
            s_ref[h] = s
            _write_top_desc(s, PEER_TOPK, top_scr)
        cand_scr[0:PEER_TOPK, :] = a_scr[0:1, :] + b_scr[...]
        for p in range(1, PEER_TOPK):
            r0 = PEER_TOPK + SUBLANES * (p - 1)
            cand_scr[r0:r0 + SUBLANES, :] = a_scr[p:p + 1, :] + b_scr[0:SUBLANES, :]
        cand = cand_scr[...]
        c = cand
        for _ in range(PEER_TOPK - 1):
            c = jnp.where(c >= jnp.max(c, axis=0, keepdims=True), NEG_BIG, c)
        tau = jnp.max(c, axis=0, keepdims=True)
        z = jnp.sum(jnp.where(cand >= tau, jnp.exp(cand), 0.0), axis=0, keepdims=True)
        tz_ref[h:h + 1, :] = tau
        tz_ref[PEER_HEADS + h:PEER_HEADS + h + 1, :] = 1.0 / z


def peer_scores(xb, wq, keys):
    m, d = xb.shape
    t = PEER_SCORE_TILE
    sblk = pl.BlockSpec((PEER_HEADS, PEER_NKEYS, t), lambda i: (0, 0, i))
    sshape = jax.ShapeDtypeStruct((PEER_HEADS, PEER_NKEYS, m), F32)
    return pl.pallas_call(
        _peer_scores_kernel,
        grid=(m // t,),
        in_specs=[pl.BlockSpec((t, d), lambda i: (i, 0)),
                  pl.BlockSpec((d, d), lambda i: (0, 0)),
                  pl.BlockSpec((2, PEER_NKEYS, PEER_HALF), lambda i: (0, 0, 0))],
        out_specs=[sblk, sblk, pl.BlockSpec((2 * PEER_HEADS, t), lambda i: (0, i))],
        out_shape=[sshape, sshape, jax.ShapeDtypeStruct((2 * PEER_HEADS, m), F32)],
        scratch_shapes=[pltpu.VMEM((PEER_TOPK, t), F32), pltpu.VMEM((PEER_TOPK, t), F32),
                        pltpu.VMEM((PEER_NCAND, t), F32)],
        compiler_params=_params("parallel"),
        name="peer_scores",
    )(xb, wq, keys)


PEER_TOKENS = 512
PEER_ETILE = 1024


def _peer_expert_kernel(xb_ref, s1_ref, s2_ref, tz_ref, u_ref, vt_ref, o_ref, acc_ref, e1_ref, e2_ref, at_ref):
    e = pl.program_id(1)

    @pl.when(e == 0)
    def _():
        acc_ref[...] = jnp.zeros_like(acc_ref)
        for h in range(PEER_HEADS):
            e1_ref[h] = jnp.exp(s1_ref[h]) * tz_ref[PEER_HEADS + h:PEER_HEADS + h + 1, :]
            e2_ref[h] = jnp.exp(s2_ref[h])

    ht = lax.dot_general(u_ref[...], xb_ref[...], (((1,), (1,)), ((), ())), preferred_element_type=F32)
    n_first = PEER_ETILE // PEER_NKEYS
    first = pl.ds(pl.multiple_of(e * n_first, n_first), n_first)
    for tb in range(PEER_TOKENS // LANES):
        cols = pl.ds(tb * LANES, LANES)
        s1_rows = [s1_ref[h, first, cols] for h in range(PEER_HEADS)]
        e1_rows = [e1_ref[h, first, cols] for h in range(PEER_HEADS)]
        for ii in range(n_first):
            gate = jnp.zeros((PEER_NKEYS, LANES), F32)
            for h in range(PEER_HEADS):
                pair = s2_ref[h, :, cols] + s1_rows[h][ii:ii + 1, :]
                w = e2_ref[h, :, cols] * e1_rows[h][ii:ii + 1, :]
                gate = gate + jnp.where(pair >= tz_ref[h:h + 1, cols], w, 0.0)
            hid = ht[ii * PEER_NKEYS:(ii + 1) * PEER_NKEYS, tb * LANES:(tb + 1) * LANES]
            at_ref[ii * PEER_NKEYS:(ii + 1) * PEER_NKEYS, cols] = (gate * _gelu(hid)).astype(BF16)
    acc_ref[...] += jnp.dot(vt_ref[...], at_ref[...], preferred_element_type=F32)

    @pl.when(e == pl.num_programs(1) - 1)
    def _():
        o_ref[...] = acc_ref[...].T


def peer_experts(xb, s1, s2, tz, u_tab, vt_tab):
    m, d = xb.shape
    t = PEER_TOKENS
    sblk = pl.BlockSpec((PEER_HEADS, PEER_NKEYS, t), lambda i, e: (0, 0, i))
    return pl.pallas_call(
        _peer_expert_kernel,
        grid=(m // t, PEER_EXPERTS // PEER_ETILE),
        in_specs=[pl.BlockSpec((t, d), lambda i, e: (i, 0)),
                  sblk, sblk,
                  pl.BlockSpec((2 * PEER_HEADS, t), lambda i, e: (0, i)),
                  pl.BlockSpec((PEER_ETILE, d), lambda i, e: (e, 0)),
                  pl.BlockSpec((d, PEER_ETILE), lambda i, e: (0, e))],
        out_specs=pl.BlockSpec((t, d), lambda i, e: (i, 0)),
        out_shape=jax.ShapeDtypeStruct((m, d), F32),
        scratch_shapes=[pltpu.VMEM((d, t), F32),
                        pltpu.VMEM((PEER_HEADS, PEER_NKEYS, t), F32),
                        pltpu.VMEM((PEER_HEADS, PEER_NKEYS, t), F32),
                        pltpu.VMEM((PEER_ETILE, t), BF16)],
        compiler_params=_params("parallel", "arbitrary"),
        name="peer_experts",
    )(xb, s1, s2, tz, u_tab, vt_tab)


MM_TILE_M = 1024
MM_TILE_N = 512


def _trunk(x, bsz, seq, p):
    m = bsz * seq
    x, xb = layer_norm_rows(x, p['ln_in_g'], p['ln_in_b'])
    cos, sin = rotary_tables(seq)
    for l in range(DEPTH):
        ua = matmul(xb, p['w_in_a'][l], F32, MM_TILE_M, S5_WIDTH, time_major_batch=(bsz, seq), name="mm_in_s5")
        ua = ua.reshape(m, S5_WIDTH)
        ub = matmul(xb, p['w_in_b'][l], F32, MM_TILE_M, MM_TILE_N, name="mm_in_pool")
        hq = matmul(xb, p['w_in_c'][l], F32, MM_TILE_M, MM_TILE_N, name="mm_in_ret")
        y_fwd = s5_direction(ua, *p['s5_fwd'][l], bsz, False)
        ya = s5_direction(ua, *p['s5_bwd'][l], bsz, True,
                          fin=(y_fwd, p['s5_d'][l], p['s5_w_glu'][l], p['s5_b_glu'][l]))
        yb = pool_mixer(ub, p['pool_w'][l], p['pool_scale'][l], seq)
        part = retention_direction(hq, cos, sin, bsz, seq, False)
        yc = retention_direction(hq, cos, sin, bsz, seq, True, part=part)
        x, xb = mix_ln(x, ya.reshape(seq, bsz * S5_WIDTH), yb, yc, p['w_out'][l],
                       p['ln1_g'][l], p['ln1_b'][l], bsz, seq)
        s1, s2, tz = peer_scores(xb, p['peer_w_q'][l], p['peer_keys'][l])
        ff = peer_experts(xb, s1, s2, tz, p['peer_u'][l], p['peer_vt'][l])
        x, xb = resid_ln(x, ff, p['ln2_g'][l], p['ln2_b'][l])
    return x


def kernel(x_prompt, x_sample, ln_in_g, ln_in_b, w_in, s5_lambda_re, s5_lambda_im, s5_log_step, s5_b_re, s5_b_im, s5_c_re, s5_c_im, s5_d, s5_w_glu, s5_b_glu, pool_w, pool_scale, w_out, ln1_g, ln1_b, peer_w_q, peer_sub_keys, peer_u, peer_v, ln2_g, ln2_b):
    d = D_MODEL
    r0 = S5_WIDTH + POOL_WIDTH
    w_in_b16 = w_in.astype(BF16)

    def s5_dir(direction):
        return [s5_weights(s5_lambda_re[l, direction], s5_lambda_im[l, direction], s5_log_step[l, direction],
                           s5_b_re[l, direction], s5_b_im[l, direction], s5_c_re[l, direction],
                           s5_c_im[l, direction]) for l in range(DEPTH)]

    p = dict(
        ln_in_g=ln_in_g, ln_in_b=ln_in_b,
        w_in_a=w_in_b16[:, :, :S5_WIDTH], w_in_b=w_in_b16[:, :, S5_WIDTH:r0], w_in_c=w_in_b16[:, :, r0:],
        s5_fwd=s5_dir(0), s5_bwd=s5_dir(1),
        s5_d=s5_d.reshape(DEPTH, 1, S5_WIDTH), s5_w_glu=s5_w_glu.astype(BF16),
        s5_b_glu=s5_b_glu.reshape(DEPTH, 1, S5_WIDTH),
        pool_w=pool_w.astype(BF16), pool_scale=pool_scale.reshape(DEPTH, 1, POOL_WIDTH),
        w_out=w_out.astype(BF16),
        ln1_g=ln1_g.reshape(DEPTH, 1, d), ln1_b=ln1_b.reshape(DEPTH, 1, d),
        peer_w_q=peer_w_q.astype(BF16), peer_keys=peer_sub_keys.astype(BF16),
        peer_u=peer_u.astype(BF16), peer_vt=jnp.swapaxes(peer_v, 1, 2).astype(BF16),
        ln2_g=ln2_g.reshape(DEPTH, 1, d), ln2_b=ln2_b.reshape(DEPTH, 1, d),
    )
    outs = []
    for x in (x_prompt, x_sample):
        bsz, seq, _ = x.shape
        outs.append(_trunk(x.reshape(bsz * seq, d), bsz, seq, p).reshape(bsz, seq, d))
    return tuple(outs)
```

```python
import functools
import math

import jax
import jax.numpy as jnp
from jax import lax
from jax.experimental import pallas as pl
from jax.experimental.pallas import tpu as pltpu

F32 = jnp.float32
BF16 = jnp.bfloat16

D_MODEL = 2048
DEPTH = 4
S5_WIDTH = 512
POOL_WIDTH = 512
RET_WIDTH = 1024
S5_GROUP = 16
S5_GROUPS = 32
S5_STATE = 64
S5_NSTATE = S5_GROUPS * S5_STATE
POOL_WINDOWS = (2, 4, 8, 16)
POOL_GROUP = 128
RET_HEADS = 4
RET_HEAD_DIM = 256
RET_CHUNK = 128
PEER_HEADS = 8
PEER_HALF = 128
PEER_NKEYS = 128
PEER_EXPERTS = PEER_NKEYS * PEER_NKEYS
PEER_TOPK = 16
ALPHA = (2 * DEPTH) ** 0.25
LN_EPS = 1e-5
GN_EPS = 1e-6

LANES = 128
SUBLANES = 8
VMEM_LIMIT = 56 * 1024 * 1024


def _params(*sem, flags=None):
    return pltpu.CompilerParams(dimension_semantics=sem, vmem_limit_bytes=VMEM_LIMIT, flags=flags)


def _layer_norm(y, g, b):
    mu = jnp.mean(y, axis=-1, keepdims=True)
    yc = y - mu
    var = jnp.mean(yc * yc, axis=-1, keepdims=True)
    return yc * lax.rsqrt(var + LN_EPS) * g + b


def _gelu(x):
    return 0.5 * x * (1.0 + lax.erf(x * (1.0 / math.sqrt(2.0))))


def _ln_kernel(x_ref, g_ref, b_ref, y_ref, yb_ref):
    y = _layer_norm(x_ref[...], g_ref[...], b_ref[...])
    y_ref[...] = y
    yb_ref[...] = y.astype(BF16)


def layer_norm_rows(x, g, b, tm=256):
    m, d = x.shape
    return pl.pallas_call(
        _ln_kernel,
        grid=(m // tm,),
        in_specs=[pl.BlockSpec((tm, d), lambda i: (i, 0)),
                  pl.BlockSpec((1, d), lambda i: (0, 0)),
                  pl.BlockSpec((1, d), lambda i: (0, 0))],
        out_specs=[pl.BlockSpec((tm, d), lambda i: (i, 0)),
                   pl.BlockSpec((tm, d), lambda i: (i, 0))],
        out_shape=[jax.ShapeDtypeStruct((m, d), F32), jax.ShapeDtypeStruct((m, d), BF16)],
        compiler_params=_params("parallel"),
        name="ln_in",
    )(x, g.reshape(1, d), b.reshape(1, d))


def _mm_kernel(x_ref, w_ref, o_ref):
    o_ref[...] = jnp.dot(x_ref[...], w_ref[...], preferred_element_type=F32).astype(o_ref.dtype)


def matmul(xb, w, out_dtype, tm, tn, time_major_batch=None, name="mm"):
    m, k = xb.shape
    n = w.shape[1]
    if time_major_batch is None:
        out_shape = jax.ShapeDtypeStruct((m, n), out_dtype)
        out_spec = pl.BlockSpec((tm, tn), lambda i, j: (i, j))
    else:
        bsz, seq = time_major_batch
        assert n == tn and seq % tm == 0 and bsz * seq == m
        tiles = seq // tm
        out_shape = jax.ShapeDtypeStruct((seq, bsz * n), out_dtype)
        out_spec = pl.BlockSpec((tm, tn), lambda i, j: (i % tiles, i // tiles))
    return pl.pallas_call(
        _mm_kernel,
        grid=(m // tm, n // tn),
        in_specs=[pl.BlockSpec((tm, k), lambda i, j: (i, 0)),
                  pl.BlockSpec((k, tn), lambda i, j: (0, j))],
        out_specs=out_spec,
        out_shape=out_shape,
        compiler_params=_params("parallel", "arbitrary"),
        name=name,
    )(xb, w)


S5_ROWS = 512
S5_COLS = 512


def _s5_kernel(*refs, bsz, reverse, finalize):
    if finalize:
        (u_ref, ar_ref, ai_ref, bbig_ref, cbig_ref, yprev_ref, d_ref, wglu_ref, bglu_ref,
         o_ref, xr_ref, xi_ref, bu_ref) = refs
    else:
        u_ref, ar_ref, ai_ref, bbig_ref, cbig_ref, o_ref, xr_ref, xi_ref, bu_ref = refs

    @pl.when(pl.program_id(0) == 0)
    def _():
        xr_ref[...] = jnp.zeros_like(xr_ref)
        xi_ref[...] = jnp.zeros_like(xi_ref)

    u = u_ref[...]
    bu_ref[...] = jnp.dot(u.astype(BF16), bbig_ref[...], preferred_element_type=F32)
    steps = S5_ROWS // bsz
    for cb in range(S5_NSTATE // S5_COLS):
        cr = pl.ds(cb * S5_COLS, S5_COLS)
        ci = pl.ds(S5_NSTATE + cb * S5_COLS, S5_COLS)
        ar = jnp.broadcast_to(ar_ref[:, cr], (bsz, S5_COLS))
        ai = jnp.broadcast_to(ai_ref[:, cr], (bsz, S5_COLS))

        def step(s, carry, cr=cr, ci=ci, ar=ar, ai=ai):
            xr, xi = carry
            t = (steps - 1 - s) if reverse else s
            rows = pl.ds(pl.multiple_of(t * bsz, bsz), bsz)
            nr = ar * xr - ai * xi + bu_ref[rows, cr]
            ni = ar * xi + ai * xr + bu_ref[rows, ci]
            bu_ref[rows, cr] = nr
            bu_ref[rows, ci] = ni
            return nr, ni

        xr, xi = lax.fori_loop(0, steps, step, (xr_ref[:, cr], xi_ref[:, cr]), unroll=8)
        xr_ref[:, cr] = xr
        xi_ref[:, cr] = xi
    y = jnp.dot(bu_ref[...].astype(BF16), cbig_ref[...], preferred_element_type=F32)
    if finalize:
        y = _gelu(y + yprev_ref[...] + d_ref[...] * u)
        z = jnp.dot(y.astype(BF16), wglu_ref[...], preferred_element_type=F32) + bglu_ref[...]
        o_ref[...] = (y * jax.nn.sigmoid(z)).astype(o_ref.dtype)
    else:
        o_ref[...] = y


def s5_direction(u, abar_r, abar_i, bbig, cbig, bsz, reverse, fin=None):
    m = u.shape[0]
    nc = m // S5_ROWS
    cidx = (lambda c: (nc - 1 - c, 0)) if reverse else (lambda c: (c, 0))
    const = lambda c: (0, 0)
    in_specs = [pl.BlockSpec((S5_ROWS, S5_WIDTH), cidx),
                pl.BlockSpec((1, S5_NSTATE), const),
                pl.BlockSpec((1, S5_NSTATE), const),
                pl.BlockSpec((S5_WIDTH, 2 * S5_NSTATE), const),
                pl.BlockSpec((2 * S5_NSTATE, S5_WIDTH), const)]
    args = [u, abar_r, abar_i, bbig, cbig]
    if fin is not None:
        y_prev, d, w_glu, b_glu = fin
        in_specs += [pl.BlockSpec((S5_ROWS, S5_WIDTH), cidx),
                     pl.BlockSpec((1, S5_WIDTH), const),
                     pl.BlockSpec((S5_WIDTH, S5_WIDTH), const),
                     pl.BlockSpec((1, S5_WIDTH), const)]
        args += [y_prev, d, w_glu, b_glu]
    return pl.pallas_call(
        functools.partial(_s5_kernel, bsz=bsz, reverse=reverse, finalize=fin is not None),
        grid=(nc,),
        in_specs=in_specs,
        out_specs=pl.BlockSpec((S5_ROWS, S5_WIDTH), cidx),
        out_shape=jax.ShapeDtypeStruct((m, S5_WIDTH), BF16 if fin is not None else F32),
        scratch_shapes=[pltpu.VMEM((bsz, S5_NSTATE), F32),
                        pltpu.VMEM((bsz, S5_NSTATE), F32),
                        pltpu.VMEM((S5_ROWS, 2 * S5_NSTATE), F32)],
        compiler_params=_params("arbitrary"),
        name="s5_bwd" if reverse else "s5_fwd",
    )(*args)


def s5_weights(lam_re, lam_im, log_step, b_re, b_im, c_re, c_im):
    dt = jnp.exp(log_step)[:, None]
    mag = jnp.exp(lam_re * dt)
    abar_r = mag * jnp.cos(lam_im * dt)
    abar_i = mag * jnp.sin(lam_im * dt)
    den = lam_re * lam_re + lam_im * lam_im
    zr = ((abar_r - 1.0) * lam_re + abar_i * lam_im) / den
    zi = (abar_i * lam_re - (abar_r - 1.0) * lam_im) / den
    bb_r = zr[..., None] * b_re - zi[..., None] * b_im
    bb_i = zr[..., None] * b_im + zi[..., None] * b_re
    eye = jnp.eye(S5_GROUPS, dtype=F32)

    def b_block(bb):
        return jnp.einsum('gpc,gh->gchp', bb, eye).reshape(S5_WIDTH, S5_NSTATE)

    def c_block(cc):
        return jnp.einsum('gcp,gh->gphc', cc, eye).reshape(S5_NSTATE, S5_WIDTH)

    bbig = jnp.concatenate([b_block(bb_r), b_block(bb_i)], axis=1).astype(BF16)
    cbig = jnp.concatenate([c_block(c_re), -c_block(c_im)], axis=0).astype(BF16)
    return abar_r.reshape(1, S5_NSTATE), abar_i.reshape(1, S5_NSTATE), bbig, cbig


POOL_TILE = 256
POOL_HALO = 64


def _pool_kernel(prev_ref, u_ref, next_ref, w_ref, scale_ref, o_ref, *, tiles_per_seq, seq_len):
    t0 = (pl.program_id(0) % tiles_per_seq) * POOL_TILE
    u = u_ref[...]
    ext = jnp.concatenate([prev_ref[...], u, next_ref[...]], axis=0)
    n_ext = POOL_TILE + 2 * POOL_HALO
    t_out = t0 + lax.broadcasted_iota(jnp.int32, (POOL_TILE, 1), 0)
    t_src = t0 - POOL_HALO + lax.broadcasted_iota(jnp.int32, (1, n_ext), 1)
    outs = []
    for gi, w in enumerate(POOL_WINDOWS):
        lo = jnp.maximum(t_out - w // 2, 0)
        hi = jnp.minimum(t_out + w // 2 - 1, seq_len - 1)
        band = jnp.where((t_src >= lo) & (t_src <= hi), 1.0, 0.0).astype(F32)
        cols = slice(gi * POOL_GROUP, (gi + 1) * POOL_GROUP)
        tot = jnp.dot(band, ext[:, cols], preferred_element_type=F32, precision=lax.Precision.HIGHEST)
        p = tot / (hi - lo + 1).astype(F32) - u[:, cols]
        outs.append(jnp.dot(p.astype(BF16), w_ref[gi], preferred_element_type=F32))
    o_ref[...] = (jnp.concatenate(outs, axis=1) * scale_ref[...]).astype(o_ref.dtype)


def pool_mixer(u, w, scale, seq_len):
    m = u.shape[0]
    tiles_per_seq = seq_len // POOL_TILE
    r = POOL_TILE // POOL_HALO
    last = m // POOL_HALO - 1
    return pl.pallas_call(
        functools.partial(_pool_kernel, tiles_per_seq=tiles_per_seq, seq_len=seq_len),
        grid=(m // POOL_TILE,),
        in_specs=[pl.BlockSpec((POOL_HALO, POOL_WIDTH), lambda i: (jnp.maximum(i * r - 1, 0), 0)),
                  pl.BlockSpec((POOL_TILE, POOL_WIDTH), lambda i: (i, 0)),
                  pl.BlockSpec((POOL_HALO, POOL_WIDTH), lambda i: (jnp.minimum((i + 1) * r, last), 0)),
                  pl.BlockSpec((len(POOL_WINDOWS), POOL_GROUP, POOL_GROUP), lambda i: (0, 0, 0)),
                  pl.BlockSpec((1, POOL_WIDTH), lambda i: (0, 0))],
        out_specs=pl.BlockSpec((POOL_TILE, POOL_WIDTH), lambda i: (i, 0)),
        out_shape=jax.ShapeDtypeStruct((m, POOL_WIDTH), BF16),
        compiler_params=_params("parallel"),
        name="pool",
    )(u, u, u, w, scale)


def _ret_kernel(*refs, reverse, finalize):
    if finalize:
        q_ref, k_ref, v_ref, cos_ref, sin_ref, g_ref, part_ref, o_ref, state_ref = refs
    else:
        q_ref, k_ref, v_ref, cos_ref, sin_ref, o_ref, state_ref = refs

    @pl.when(pl.program_id(1) == 0)
    def _():
        state_ref[...] = jnp.zeros_like(state_ref)

    cos = cos_ref[...]
    sin = sin_ref[...]
    d = RET_HEAD_DIM
    half = d // 2
    n = RET_CHUNK

    def rotary(x):
        x1, x2 = x[:, :half], x[:, half:]
        return jnp.concatenate([x1 * cos - x2 * sin, x2 * cos + x1 * sin], axis=1)

    pos = lax.broadcasted_iota(jnp.int32, (n, 1), 0).astype(F32)
    dist = jnp.abs(lax.broadcasted_iota(jnp.int32, (n, n), 0)
                   - lax.broadcasted_iota(jnp.int32, (n, n), 1)).astype(F32)
    for h in range(RET_HEADS):
        cs = slice(h * d, (h + 1) * d)
        log_gamma = math.log(1.0 - 2.0 ** (-5.0 - h))
        q = rotary(q_ref[:, cs])
        k = rotary(k_ref[:, cs]) * (d ** -0.5)
        vb = v_ref[:, cs].astype(BF16)
        if reverse:
            q_decay = jnp.exp(log_gamma * (n - pos))
            k_decay = jnp.exp(log_gamma * pos)
        else:
            q_decay = jnp.exp(log_gamma * (pos + 1.0))
            k_decay = jnp.exp(log_gamma * (n - 1.0 - pos))
        state = state_ref[h]
        o = jnp.dot((q * q_decay).astype(BF16), state.astype(BF16), preferred_element_type=F32)
        if not reverse:
            scores = lax.dot_general(q.astype(BF16), k.astype(BF16), (((1,), (1,)), ((), ())),
                                     preferred_element_type=F32) * jnp.exp(log_gamma * dist)
            o = o + jnp.dot(scores.astype(BF16), vb, preferred_element_type=F32)
        kv = lax.dot_general((k * k_decay).astype(BF16), vb, (((0,), (0,)), ((), ())),
                             preferred_element_type=F32)
        state_ref[h] = math.exp(log_gamma * n) * state + kv
        if finalize:
            o = o + part_ref[:, cs]
            mu = jnp.mean(o, axis=-1, keepdims=True)
            oc = o - mu
            var = jnp.mean(oc * oc, axis=-1, keepdims=True)
            g = g_ref[:, cs]
            o_ref[:, cs] = (g * jax.nn.sigmoid(g) * (oc * lax.rsqrt(var + GN_EPS))).astype(o_ref.dtype)
        else:
            o_ref[:, cs] = o


def retention_direction(hq, cos, sin, bsz, seq, reverse, part=None):
    m = hq.shape[0]
    nc = seq // RET_CHUNK
    w = RET_WIDTH

    def rows(b, c):
        return b * nc + ((nc - 1 - c) if reverse else c)

    def col(j):
        return pl.BlockSpec((RET_CHUNK, w), lambda b, c, j=j: (rows(b, c), j))

    tab = pl.BlockSpec((RET_CHUNK, RET_HEAD_DIM // 2), lambda b, c: ((nc - 1 - c) if reverse else c, 0))
    in_specs = [col(0), col(1), col(2), tab, tab]
    args = [hq, hq, hq, cos, sin]
    if part is not None:
        in_specs += [col(3), col(0)]
        args += [hq, part]
    return pl.pallas_call(
        functools.partial(_ret_kernel, reverse=reverse, finalize=part is not None),
        grid=(bsz, nc),
        in_specs=in_specs,
        out_specs=col(0),
        out_shape=jax.ShapeDtypeStruct((m, w), BF16 if part is not None else F32),
        scratch_shapes=[pltpu.VMEM((RET_HEADS, RET_HEAD_DIM, RET_HEAD_DIM), F32)],
        compiler_params=_params("parallel", "arbitrary"),
        name="ret_bwd" if reverse else "ret_fwd",
    )(*args)


def rotary_tables(seq):
    half = RET_HEAD_DIM // 2
    inv = 1.0 / (10000.0 ** jnp.linspace(0.0, 1.0, half, dtype=F32))
    ang = jnp.arange(seq, dtype=F32)[:, None] * inv[None, :]
    return jnp.cos(ang), jnp.sin(ang)


MIX_TILE = 256


def _mix_kernel(x_ref, ya_ref, yb_ref, yc_ref, wa_ref, wb_ref, wc_ref, g_ref, b_ref, o_ref, ob_ref):
    mix = (jnp.dot(ya_ref[...], wa_ref[...], preferred_element_type=F32)
           + jnp.dot(yb_ref[...], wb_ref[...], preferred_element_type=F32)
           + jnp.dot(yc_ref[...], wc_ref[...], preferred_element_type=F32))
    y = _layer_norm(ALPHA * x_ref[...] + mix, g_ref[...], b_ref[...])
    o_ref[...] = y
    ob_ref[...] = y.astype(BF16)


def mix_ln(x, ya_tb, yb, yc, w_out, g, b, bsz, seq):
    m, d = x.shape
    tm = MIX_TILE
    tiles = seq // tm
    row = lambda i: (i, 0)
    const = lambda i: (0, 0)
    return pl.pallas_call(
        _mix_kernel,
        grid=(m // tm,),
        in_specs=[pl.BlockSpec((tm, d), row),
                  pl.BlockSpec((tm, S5_WIDTH), lambda i: (i % tiles, i // tiles)),
                  pl.BlockSpec((tm, POOL_WIDTH), row),
                  pl.BlockSpec((tm, RET_WIDTH), row),
                  pl.BlockSpec((S5_WIDTH, d), lambda i: (0, 0)),
                  pl.BlockSpec((POOL_WIDTH, d), lambda i: (1, 0)),
                  pl.BlockSpec((RET_WIDTH, d), lambda i: (1, 0)),
                  pl.BlockSpec((1, d), const),
                  pl.BlockSpec((1, d), const)],
        out_specs=[pl.BlockSpec((tm, d), row), pl.BlockSpec((tm, d), row)],
        out_shape=[jax.ShapeDtypeStruct((m, d), F32), jax.ShapeDtypeStruct((m, d), BF16)],
        compiler_params=_params("parallel"),
        name="mix_ln",
    )(x, ya_tb, yb, yc, w_out, w_out, w_out, g, b)


def _resid_ln_kernel(x_ref, f_ref, g_ref, b_ref, o_ref, ob_ref):
    y = _layer_norm(ALPHA * x_ref[...] + f_ref[...], g_ref[...], b_ref[...])
    o_ref[...] = y
    ob_ref[...] = y.astype(BF16)


def resid_ln(x, f, g, b, tm=256):
    m, d = x.shape
    row = lambda i: (i, 0)
    const = lambda i: (0, 0)
    return pl.pallas_call(
        _resid_ln_kernel,
        grid=(m // tm,),
        in_specs=[pl.BlockSpec((tm, d), row), pl.BlockSpec((tm, d), row),
                  pl.BlockSpec((1, d), const), pl.BlockSpec((1, d), const)],
        out_specs=[pl.BlockSpec((tm, d), row), pl.BlockSpec((tm, d), row)],
        out_shape=[jax.ShapeDtypeStruct((m, d), F32), jax.ShapeDtypeStruct((m, d), BF16)],
        compiler_params=_params("parallel"),
        name="resid_ln",
    )(x, f, g, b)


PEER_SCORE_TILE = 256
NEG_BIG = -3.0e38
PEER_NCAND = PEER_TOPK + SUBLANES * (PEER_TOPK - 1)


def _top_desc(s, n, out_ref, with_rank):
    rank = jnp.full(s.shape, float(n), F32) if with_rank else None
    for r in range(n):
        m = jnp.max(s, axis=0, keepdims=True)
        out_ref[r:r + 1, :] = m
        hit = s >= m
        if with_rank:
            rank = jnp.where(hit, float(r), rank)
        if r < n - 1:
            s = jnp.where(hit, NEG_BIG, s)
    return rank


def _peer_scores_kernel(xb_ref, wq_ref, keys_ref, e1_ref, cnt_ref, e2_ref, r2_ref, a_scr, b_scr, cand_scr):
    q = jnp.dot(xb_ref[...], wq_ref[...], preferred_element_type=F32)
    nt = (((1,), (1,)), ((), ()))
    for h in range(PEER_HEADS):
        c0 = 2 * h * PEER_HALF
        s1 = lax.dot_general(keys_ref[0], q[:, c0:c0 + PEER_HALF].astype(BF16), nt, preferred_element_type=F32)
        s2 = lax.dot_general(keys_ref[1], q[:, c0 + PEER_HALF:c0 + 2 * PEER_HALF].astype(BF16), nt,
                             preferred_element_type=F32)
        s1 = s1 - jnp.max(s1, axis=0, keepdims=True)
        s2 = s2 - jnp.max(s2, axis=0, keepdims=True)
        _top_desc(s1, PEER_TOPK, a_scr, False)
        rank2 = _top_desc(s2, PEER_TOPK, b_scr, True)
        cand_scr[0:PEER_TOPK, :] = a_scr[0:1, :] + b_scr[...]
        for p in range(1, PEER_TOPK):
            r0 = PEER_TOPK + SUBLANES * (p - 1)
            cand_scr[r0:r0 + SUBLANES, :] = a_scr[p:p + 1, :] + b_scr[0:SUBLANES, :]
        cand = cand_scr[...]
        c = cand
        for _ in range(PEER_TOPK - 1):
            c = jnp.where(c >= jnp.max(c, axis=0, keepdims=True), NEG_BIG, c)
        tau = jnp.max(c, axis=0, keepdims=True)
        z = jnp.sum(jnp.where(cand >= tau, jnp.exp(cand), 0.0), axis=0, keepdims=True)
        b_top = b_scr[...]
        cnt = jnp.zeros(s1.shape, F32)
        for p in range(PEER_TOPK):
            a_p = a_scr[p:p + 1, :]
            c_p = jnp.sum(jnp.where(a_p + b_top >= tau, 1.0, 0.0), axis=0, keepdims=True)
            cnt = jnp.where(s1 == a_p, c_p, cnt)
        e1_ref[h] = jnp.exp(s1) * (1.0 / z)
        cnt_ref[h] = cnt
        e2_ref[h] = jnp.exp(s2).astype(BF16)
        r2_ref[h] = rank2.astype(BF16)


def peer_scores(xb, wq, keys):
    m, d = xb.shape
    t = PEER_SCORE_TILE
    sblk = pl.BlockSpec((PEER_HEADS, PEER_NKEYS, t), lambda i: (0, 0, i))
    sshape = (PEER_HEADS, PEER_NKEYS, m)
    return pl.pallas_call(
        _peer_scores_kernel,
        grid=(m // t,),
        in_specs=[pl.BlockSpec((t, d), lambda i: (i, 0)),
                  pl.BlockSpec((d, d), lambda i: (0, 0)),
                  pl.BlockSpec((2, PEER_NKEYS, PEER_HALF), lambda i: (0, 0, 0))],
        out_specs=[sblk, sblk, sblk, sblk],
        out_shape=[jax.ShapeDtypeStruct(sshape, F32), jax.ShapeDtypeStruct(sshape, F32),
                   jax.ShapeDtypeStruct(sshape, BF16), jax.ShapeDtypeStruct(sshape, BF16)],
        scratch_shapes=[pltpu.VMEM((PEER_TOPK, t), F32), pltpu.VMEM((PEER_TOPK, t), F32),
                        pltpu.VMEM((PEER_NCAND, t), F32)],
        compiler_params=_params("parallel"),
        name="peer_scores",
    )(xb, wq, keys)


PEER_TOKENS = 512
PEER_ETILE = 1024


PEER_FIRST_KEYS = PEER_ETILE // PEER_NKEYS
assert PEER_FIRST_KEYS == SUBLANES
BF16_ROWS = 2 * SUBLANES


def _peer_expert_kernel(xb_ref, e1_ref, cnt_ref, e2_in_ref, r2_in_ref, u_ref, vt_ref, o_ref,
                        acc_ref, at_ref, e2_ref, r2_ref):
    e = pl.program_id(1)

    @pl.when(e == 0)
    def _():
        acc_ref[...] = jnp.zeros_like(acc_ref)
        e2_ref[...] = e2_in_ref[...]
        r2_ref[...] = r2_in_ref[...]

    ht = lax.dot_general(u_ref[...], xb_ref[...], (((1,), (1,)), ((), ())), preferred_element_type=F32)
    at_ref[...] = _gelu(ht).astype(BF16)
    first = pl.ds(pl.multiple_of(e * PEER_FIRST_KEYS, PEER_FIRST_KEYS), PEER_FIRST_KEYS)
    for tb in range(PEER_TOKENS // LANES):
        cols = pl.ds(tb * LANES, LANES)
        cnt_rows = [cnt_ref[h, first, cols] for h in range(PEER_HEADS)]
        e1_rows = [e1_ref[h, first, cols] for h in range(PEER_HEADS)]
        for ii in range(PEER_FIRST_KEYS):
            cnt_i = [jnp.broadcast_to(cnt_rows[h][ii:ii + 1, :], (BF16_ROWS, LANES)).astype(BF16)
                     for h in range(PEER_HEADS)]
            e1_i = [jnp.broadcast_to(e1_rows[h][ii:ii + 1, :], (BF16_ROWS, LANES)).astype(BF16)
                    for h in range(PEER_HEADS)]
            for jb in range(PEER_NKEYS // BF16_ROWS):
                js = pl.ds(jb * BF16_ROWS, BF16_ROWS)
                gate = jnp.zeros((BF16_ROWS, LANES), BF16)
                for h in range(PEER_HEADS):
                    w = e2_ref[h, js, cols] * e1_i[h]
                    gate = gate + jnp.where(r2_ref[h, js, cols] < cnt_i[h], w, jnp.zeros_like(w))
                rows = pl.ds(ii * PEER_NKEYS + jb * BF16_ROWS, BF16_ROWS)
                at_ref[rows, cols] = at_ref[rows, cols] * gate
    acc_ref[...] += jnp.dot(vt_ref[...], at_ref[...], preferred_element_type=F32)

    @pl.when(e == pl.num_programs(1) - 1)
    def _():
        o_ref[...] = acc_ref[...].T


def peer_experts(xb, e1, cnt, e2, r2, u_tab, vt_tab):
    m, d = xb.shape
    t = PEER_TOKENS
    sblk = pl.BlockSpec((PEER_HEADS, PEER_NKEYS, t), lambda i, e: (0, 0, i))
    return pl.pallas_call(
        _peer_expert_kernel,
        grid=(m // t, PEER_EXPERTS // PEER_ETILE),
        in_specs=[pl.BlockSpec((t, d), lambda i, e: (i, 0)),
                  sblk, sblk, sblk, sblk,
                  pl.BlockSpec((PEER_ETILE, d), lambda i, e: (e, 0)),
                  pl.BlockSpec((d, PEER_ETILE), lambda i, e: (0, e))],
        out_specs=pl.BlockSpec((t, d), lambda i, e: (i, 0)),
        out_shape=jax.ShapeDtypeStruct((m, d), F32),
        scratch_shapes=[pltpu.VMEM((d, t), F32),
                        pltpu.VMEM((PEER_ETILE, t), BF16),
                        pltpu.VMEM((PEER_HEADS, PEER_NKEYS, t), BF16),
                        pltpu.VMEM((PEER_HEADS, PEER_NKEYS, t), BF16)],
        compiler_params=_params("parallel", "arbitrary"),
        name="peer_experts",
    )(xb, e1, cnt, e2, r2, u_tab, vt_tab)


MM_TILE_M = 1024
MM_TILE_N = 512


def _trunk(x, bsz, seq, p):
    m = bsz * seq
    x, xb = layer_norm_rows(x, p['ln_in_g'], p['ln_in_b'])
    cos, sin = rotary_tables(seq)
    for l in range(DEPTH):
        ua = matmul(xb, p['w_in_a'][l], F32, MM_TILE_M, S5_WIDTH, time_major_batch=(bsz, seq), name="mm_in_s5")
        ua = ua.reshape(m, S5_WIDTH)
        ub = matmul(xb, p['w_in_b'][l], F32, MM_TILE_M, MM_TILE_N, name="mm_in_pool")
        hq = matmul(xb, p['w_in_c'][l], F32, MM_TILE_M, MM_TILE_N, name="mm_in_ret")
        y_fwd = s5_direction(ua, *p['s5_fwd'][l], bsz, False)
        ya = s5_direction(ua, *p['s5_bwd'][l], bsz, True,
                          fin=(y_fwd, p['s5_d'][l], p['s5_w_glu'][l], p['s5_b_glu'][l]))
        yb = pool_mixer(ub, p['pool_w'][l], p['pool_scale'][l], seq)
        part = retention_direction(hq, cos, sin, bsz, seq, False)
        yc = retention_direction(hq, cos, sin, bsz, seq, True, part=part)
        x, xb = mix_ln(x, ya.reshape(seq, bsz * S5_WIDTH), yb, yc, p['w_out'][l],
                       p['ln1_g'][l], p['ln1_b'][l], bsz, seq)
        e1, cnt, e2, r2 = peer_scores(xb, p['peer_w_q'][l], p['peer_keys'][l])
        ff = peer_experts(xb, e1, cnt, e2, r2, p['peer_u'][l], p['peer_vt'][l])
        x, xb = resid_ln(x, ff, p['ln2_g'][l], p['ln2_b'][l])
    return x


def kernel(x_prompt, x_sample, ln_in_g, ln_in_b, w_in, s5_lambda_re, s5_lambda_im, s5_log_step, s5_b_re, s5_b_im, s5_c_re, s5_c_im, s5_d, s5_w_glu, s5_b_glu, pool_w, pool_scale, w_out, ln1_g, ln1_b, peer_w_q, peer_sub_keys, peer_u, peer_v, ln2_g, ln2_b):
    d = D_MODEL
    r0 = S5_WIDTH + POOL_WIDTH
    w_in_b16 = w_in.astype(BF16)

    def s5_dir(direction):
        return [s5_weights(s5_lambda_re[l, direction], s5_lambda_im[l, direction], s5_log_step[l, direction],
                           s5_b_re[l, direction], s5_b_im[l, direction], s5_c_re[l, direction],
                           s5_c_im[l, direction]) for l in range(DEPTH)]

    p = dict(
        ln_in_g=ln_in_g, ln_in_b=ln_in_b,
        w_in_a=w_in_b16[:, :, :S5_WIDTH], w_in_b=w_in_b16[:, :, S5_WIDTH:r0], w_in_c=w_in_b16[:, :, r0:],
        s5_fwd=s5_dir(0), s5_bwd=s5_dir(1),
        s5_d=s5_d.reshape(DEPTH, 1, S5_WIDTH), s5_w_glu=s5_w_glu.astype(BF16),
        s5_b_glu=s5_b_glu.reshape(DEPTH, 1, S5_WIDTH),
        pool_w=pool_w.astype(BF16), pool_scale=pool_scale.reshape(DEPTH, 1, POOL_WIDTH),
        w_out=w_out.astype(BF16),
        ln1_g=ln1_g.reshape(DEPTH, 1, d), ln1_b=ln1_b.reshape(DEPTH, 1, d),
        peer_w_q=peer_w_q.astype(BF16), peer_keys=peer_sub_keys.astype(BF16),
        peer_u=peer_u.astype(BF16), peer_vt=jnp.swapaxes(peer_v, 1, 2).astype(BF16),
        ln2_g=ln2_g.reshape(DEPTH, 1, d), ln2_b=ln2_b.reshape(DEPTH, 1, d),
    )
    outs = []
    for x in (x_prompt, x_sample):
        bsz, seq, _ = x.shape
        outs.append(_trunk(x.reshape(bsz * seq, d), bsz, seq, p).reshape(bsz, seq, d))
    return tuple(outs)
```

```python
import functools
import math

import jax
import jax.numpy as jnp
from jax import lax
from jax.experimental import pallas as pl
from jax.experimental.pallas import tpu as pltpu

F32 = jnp.float32
BF16 = jnp.bfloat16

D_MODEL = 2048
DEPTH = 4
S5_WIDTH = 512
POOL_WIDTH = 512
RET_WIDTH = 1024
S5_GROUP = 16
S5_GROUPS = 32
S5_STATE = 64
S5_NSTATE = S5_GROUPS * S5_STATE
POOL_WINDOWS = (2, 4, 8, 16)
POOL_GROUP = 128
RET_HEADS = 4
RET_HEAD_DIM = 256
RET_CHUNK = 128
PEER_HEADS = 8
PEER_HALF = 128
PEER_NKEYS = 128
PEER_EXPERTS = PEER_NKEYS * PEER_NKEYS
PEER_TOPK = 16
ALPHA = (2 * DEPTH) ** 0.25
LN_EPS = 1e-5
GN_EPS = 1e-6

LANES = 128
SUBLANES = 8
VMEM_LIMIT = 56 * 1024 * 1024


def _params(*sem, flags=None):
    return pltpu.CompilerParams(dimension_semantics=sem, vmem_limit_bytes=VMEM_LIMIT, flags=flags)


def _layer_norm(y, g, b):
    mu = jnp.mean(y, axis=-1, keepdims=True)
    yc = y - mu
    var = jnp.mean(yc * yc, axis=-1, keepdims=True)
    return yc * lax.rsqrt(var + LN_EPS) * g + b


def _gelu(x):
    return 0.5 * x * (1.0 + lax.erf(x * (1.0 / math.sqrt(2.0))))


def _ln_kernel(x_ref, g_ref, b_ref, y_ref, yb_ref):
    y = _layer_norm(x_ref[...], g_ref[...], b_ref[...])
    y_ref[...] = y
    yb_ref[...] = y.astype(BF16)


def layer_norm_rows(x, g, b, tm=256):
    m, d = x.shape
    return pl.pallas_call(
        _ln_kernel,
        grid=(m // tm,),
        in_specs=[pl.BlockSpec((tm, d), lambda i: (i, 0)),
                  pl.BlockSpec((1, d), lambda i: (0, 0)),
                  pl.BlockSpec((1, d), lambda i: (0, 0))],
        out_specs=[pl.BlockSpec((tm, d), lambda i: (i, 0)),
                   pl.BlockSpec((tm, d), lambda i: (i, 0))],
        out_shape=[jax.ShapeDtypeStruct((m, d), F32), jax.ShapeDtypeStruct((m, d), BF16)],
        compiler_params=_params("parallel"),
        name="ln_in",
    )(x, g.reshape(1, d), b.reshape(1, d))


def _mm_kernel(x_ref, w_ref, o_ref):
    o_ref[...] = jnp.dot(x_ref[...], w_ref[...], preferred_element_type=F32).astype(o_ref.dtype)


def matmul(xb, w, col0, n, out_dtype, tm, tn, time_major_batch=None, name="mm"):
    m, k = xb.shape
    assert col0 % tn == 0 and n % tn == 0
    j0 = col0 // tn
    if time_major_batch is None:
        out_shape = jax.ShapeDtypeStruct((m, n), out_dtype)
        out_spec = pl.BlockSpec((tm, tn), lambda i, j: (i, j))
    else:
        bsz, seq = time_major_batch
        assert n == tn and seq % tm == 0 and bsz * seq == m
        tiles = seq // tm
        out_shape = jax.ShapeDtypeStruct((seq, bsz * n), out_dtype)
        out_spec = pl.BlockSpec((tm, tn), lambda i, j: (i % tiles, i // tiles))
    return pl.pallas_call(
        _mm_kernel,
        grid=(m // tm, n // tn),
        in_specs=[pl.BlockSpec((tm, k), lambda i, j: (i, 0)),
                  pl.BlockSpec((k, tn), lambda i, j: (0, j0 + j))],
        out_specs=out_spec,
        out_shape=out_shape,
        compiler_params=_params("parallel", "arbitrary"),
        name=name,
    )(xb, w)


S5_ROWS = 512
S5_COLS = 512


def _s5_kernel(*refs, bsz, reverse, finalize):
    if finalize:
        (u_ref, ar_ref, ai_ref, bbig_ref, cbig_ref, yprev_ref, d_ref, wglu_ref, bglu_ref,
         o_ref, xr_ref, xi_ref, bu_ref) = refs
    else:
        u_ref, ar_ref, ai_ref, bbig_ref, cbig_ref, o_ref, xr_ref, xi_ref, bu_ref = refs

    @pl.when(pl.program_id(0) == 0)
    def _():
        xr_ref[...] = jnp.zeros_like(xr_ref)
        xi_ref[...] = jnp.zeros_like(xi_ref)

    u = u_ref[...]
    bu_ref[...] = jnp.dot(u.astype(BF16), bbig_ref[...], preferred_element_type=F32)
    steps = S5_ROWS // bsz
    for cb in range(S5_NSTATE // S5_COLS):
        cr = pl.ds(cb * S5_COLS, S5_COLS)
        ci = pl.ds(S5_NSTATE + cb * S5_COLS, S5_COLS)
        ar = jnp.broadcast_to(ar_ref[:, cr], (bsz, S5_COLS))
        ai = jnp.broadcast_to(ai_ref[:, cr], (bsz, S5_COLS))

        def step(s, carry, cr=cr, ci=ci, ar=ar, ai=ai):
            xr, xi = carry
            t = (steps - 1 - s) if reverse else s
            rows = pl.ds(pl.multiple_of(t * bsz, bsz), bsz)
            nr = ar * xr - ai * xi + bu_ref[rows, cr]
            ni = ar * xi + ai * xr + bu_ref[rows, ci]
            bu_ref[rows, cr] = nr
            bu_ref[rows, ci] = ni
            return nr, ni

        xr, xi = lax.fori_loop(0, steps, step, (xr_ref[:, cr], xi_ref[:, cr]), unroll=8)
        xr_ref[:, cr] = xr
        xi_ref[:, cr] = xi
    y = jnp.dot(bu_ref[...].astype(BF16), cbig_ref[...], preferred_element_type=F32)
    if finalize:
        y = _gelu(y + yprev_ref[...] + d_ref[...] * u)
        z = jnp.dot(y.astype(BF16), wglu_ref[...], preferred_element_type=F32) + bglu_ref[...]
        o_ref[...] = (y * jax.nn.sigmoid(z)).astype(o_ref.dtype)
    else:
        o_ref[...] = y


def s5_direction(u, abar_r, abar_i, bbig, cbig, bsz, reverse, fin=None):
    m = u.shape[0]
    nc = m // S5_ROWS
    cidx = (lambda c: (nc - 1 - c, 0)) if reverse else (lambda c: (c, 0))
    const = lambda c: (0, 0)
    in_specs = [pl.BlockSpec((S5_ROWS, S5_WIDTH), cidx),
                pl.BlockSpec((1, S5_NSTATE), const),
                pl.BlockSpec((1, S5_NSTATE), const),
                pl.BlockSpec((S5_WIDTH, 2 * S5_NSTATE), const),
                pl.BlockSpec((2 * S5_NSTATE, S5_WIDTH), const)]
    args = [u, abar_r, abar_i, bbig, cbig]
    if fin is not None:
        y_prev, d, w_glu, b_glu = fin
        in_specs += [pl.BlockSpec((S5_ROWS, S5_WIDTH), cidx),
                     pl.BlockSpec((1, S5_WIDTH), const),
                     pl.BlockSpec((S5_WIDTH, S5_WIDTH), const),
                     pl.BlockSpec((1, S5_WIDTH), const)]
        args += [y_prev, d, w_glu, b_glu]
    return pl.pallas_call(
        functools.partial(_s5_kernel, bsz=bsz, reverse=reverse, finalize=fin is not None),
        grid=(nc,),
        in_specs=in_specs,
        out_specs=pl.BlockSpec((S5_ROWS, S5_WIDTH), cidx),
        out_shape=jax.ShapeDtypeStruct((m, S5_WIDTH), BF16 if fin is not None else F32),
        scratch_shapes=[pltpu.VMEM((bsz, S5_NSTATE), F32),
                        pltpu.VMEM((bsz, S5_NSTATE), F32),
                        pltpu.VMEM((S5_ROWS, 2 * S5_NSTATE), F32)],
        compiler_params=_params("arbitrary"),
        name="s5_bwd" if reverse else "s5_fwd",
    )(*args)


def s5_weights(lam_re, lam_im, log_step, b_re, b_im, c_re, c_im):
    dt = jnp.exp(log_step)[:, None]
    mag = jnp.exp(lam_re * dt)
    abar_r = mag * jnp.cos(lam_im * dt)
    abar_i = mag * jnp.sin(lam_im * dt)
    den = lam_re * lam_re + lam_im * lam_im
    zr = ((abar_r - 1.0) * lam_re + abar_i * lam_im) / den
    zi = (abar_i * lam_re - (abar_r - 1.0) * lam_im) / den
    bb_r = zr[..., None] * b_re - zi[..., None] * b_im
    bb_i = zr[..., None] * b_im + zi[..., None] * b_re
    eye = jnp.eye(S5_GROUPS, dtype=F32)

    def b_block(bb):
        return jnp.einsum('gpc,gh->gchp', bb, eye).reshape(S5_WIDTH, S5_NSTATE)

    def c_block(cc):
        return jnp.einsum('gcp,gh->gphc', cc, eye).reshape(S5_NSTATE, S5_WIDTH)

    bbig = jnp.concatenate([b_block(bb_r), b_block(bb_i)], axis=1).astype(BF16)
    cbig = jnp.concatenate([c_block(c_re), -c_block(c_im)], axis=0).astype(BF16)
    return abar_r.reshape(1, S5_NSTATE), abar_i.reshape(1, S5_NSTATE), bbig, cbig


POOL_TILE = 256
POOL_HALO = 64


def _pool_kernel(prev_ref, u_ref, next_ref, w_ref, scale_ref, o_ref, *, tiles_per_seq, seq_len):
    t0 = (pl.program_id(0) % tiles_per_seq) * POOL_TILE
    u = u_ref[...]
    ext = jnp.concatenate([prev_ref[...], u, next_ref[...]], axis=0)
    n_ext = POOL_TILE + 2 * POOL_HALO
    t_out = t0 + lax.broadcasted_iota(jnp.int32, (POOL_TILE, 1), 0)
    t_src = t0 - POOL_HALO + lax.broadcasted_iota(jnp.int32, (1, n_ext), 1)
    outs = []
    for gi, w in enumerate(POOL_WINDOWS):
        lo = jnp.maximum(t_out - w // 2, 0)
        hi = jnp.minimum(t_out + w // 2 - 1, seq_len - 1)
        band = jnp.where((t_src >= lo) & (t_src <= hi), 1.0, 0.0).astype(BF16)
        cols = slice(gi * POOL_GROUP, (gi + 1) * POOL_GROUP)
        head = ext[:, cols].astype(BF16)
        rest = (ext[:, cols] - head.astype(F32)).astype(BF16)
        tot = (jnp.dot(band, head, preferred_element_type=F32) + jnp.dot(band, rest, preferred_element_type=F32))
        p = tot / (hi - lo + 1).astype(F32) - u[:, cols]
        outs.append(jnp.dot(p.astype(BF16), w_ref[gi], preferred_element_type=F32))
    o_ref[...] = (jnp.concatenate(outs, axis=1) * scale_ref[...]).astype(o_ref.dtype)


def pool_mixer(u, w, scale, seq_len):
    m = u.shape[0]
    tiles_per_seq = seq_len // POOL_TILE
    r = POOL_TILE // POOL_HALO
    last = m // POOL_HALO - 1
    return pl.pallas_call(
        functools.partial(_pool_kernel, tiles_per_seq=tiles_per_seq, seq_len=seq_len),
        grid=(m // POOL_TILE,),
        in_specs=[pl.BlockSpec((POOL_HALO, POOL_WIDTH), lambda i: (jnp.maximum(i * r - 1, 0), 0)),
                  pl.BlockSpec((POOL_TILE, POOL_WIDTH), lambda i: (i, 0)),
                  pl.BlockSpec((POOL_HALO, POOL_WIDTH), lambda i: (jnp.minimum((i + 1) * r, last), 0)),
                  pl.BlockSpec((len(POOL_WINDOWS), POOL_GROUP, POOL_GROUP), lambda i: (0, 0, 0)),
                  pl.BlockSpec((1, POOL_WIDTH), lambda i: (0, 0))],
        out_specs=pl.BlockSpec((POOL_TILE, POOL_WIDTH), lambda i: (i, 0)),
        out_shape=jax.ShapeDtypeStruct((m, POOL_WIDTH), BF16),
        compiler_params=_params("parallel"),
        name="pool",
    )(u, u, u, w, scale)


def _ret_kernel(*refs, reverse, finalize):
    if finalize:
        q_ref, k_ref, v_ref, cos_ref, sin_ref, g_ref, part_ref, o_ref, state_ref = refs
    else:
        q_ref, k_ref, v_ref, cos_ref, sin_ref, o_ref, state_ref = refs

    @pl.when(pl.program_id(1) == 0)
    def _():
        state_ref[...] = jnp.zeros_like(state_ref)

    cos = cos_ref[...]
    sin = sin_ref[...]
    d = RET_HEAD_DIM
    half = d // 2
    n = RET_CHUNK

    def rotary(x):
        x1, x2 = x[:, :half], x[:, half:]
        return jnp.concatenate([x1 * cos - x2 * sin, x2 * cos + x1 * sin], axis=1)

    pos = lax.broadcasted_iota(jnp.int32, (n, 1), 0).astype(F32)
    dist = jnp.abs(lax.broadcasted_iota(jnp.int32, (n, n), 0)
                   - lax.broadcasted_iota(jnp.int32, (n, n), 1)).astype(F32)
    for h in range(RET_HEADS):
        cs = slice(h * d, (h + 1) * d)
        log_gamma = math.log(1.0 - 2.0 ** (-5.0 - h))
        q = rotary(q_ref[:, cs].astype(F32))
        k = rotary(k_ref[:, cs].astype(F32)) * (d ** -0.5)
        vb = v_ref[:, cs]
        if reverse:
            q_decay = jnp.exp(log_gamma * (n - pos))
            k_decay = jnp.exp(log_gamma * pos)
        else:
            q_decay = jnp.exp(log_gamma * (pos + 1.0))
            k_decay = jnp.exp(log_gamma * (n - 1.0 - pos))
        state = state_ref[h]
        o = jnp.dot((q * q_decay).astype(BF16), state.astype(BF16), preferred_element_type=F32)
        if not reverse:
            scores = lax.dot_general(q.astype(BF16), k.astype(BF16), (((1,), (1,)), ((), ())),
                                     preferred_element_type=F32) * jnp.exp(log_gamma * dist)
            o = o + jnp.dot(scores.astype(BF16), vb, preferred_element_type=F32)
        kv = lax.dot_general((k * k_decay).astype(BF16), vb, (((0,), (0,)), ((), ())),
                             preferred_element_type=F32)
        state_ref[h] = math.exp(log_gamma * n) * state + kv
        if finalize:
            o = o + part_ref[:, cs]
            mu = jnp.mean(o, axis=-1, keepdims=True)
            oc = o - mu
            var = jnp.mean(oc * oc, axis=-1, keepdims=True)
            g = g_ref[:, cs].astype(F32)
            o_ref[:, cs] = (g * jax.nn.sigmoid(g) * (oc * lax.rsqrt(var + GN_EPS))).astype(o_ref.dtype)
        else:
            o_ref[:, cs] = o


def retention_direction(hq, cos, sin, bsz, seq, reverse, part=None):
    m = hq.shape[0]
    nc = seq // RET_CHUNK
    w = RET_WIDTH

    def rows(b, c):
        return b * nc + ((nc - 1 - c) if reverse else c)

    def col(j):
        return pl.BlockSpec((RET_CHUNK, w), lambda b, c, j=j: (rows(b, c), j))

    tab = pl.BlockSpec((RET_CHUNK, RET_HEAD_DIM // 2), lambda b, c: ((nc - 1 - c) if reverse else c, 0))
    in_specs = [col(0), col(1), col(2), tab, tab]
    args = [hq, hq, hq, cos, sin]
    if part is not None:
        in_specs += [col(3), col(0)]
        args += [hq, part]
    return pl.pallas_call(
        functools.partial(_ret_kernel, reverse=reverse, finalize=part is not None),
        grid=(bsz, nc),
        in_specs=in_specs,
        out_specs=col(0),
        out_shape=jax.ShapeDtypeStruct((m, w), BF16 if part is not None else F32),
        scratch_shapes=[pltpu.VMEM((RET_HEADS, RET_HEAD_DIM, RET_HEAD_DIM), F32)],
        compiler_params=_params("parallel", "arbitrary"),
        name="ret_bwd" if reverse else "ret_fwd",
    )(*args)


def rotary_tables(seq):
    half = RET_HEAD_DIM // 2
    inv = 1.0 / (10000.0 ** jnp.linspace(0.0, 1.0, half, dtype=F32))
    ang = jnp.arange(seq, dtype=F32)[:, None] * inv[None, :]
    return jnp.cos(ang), jnp.sin(ang)


MIX_TILE = 256


def _mix_kernel(x_ref, ya_ref, yb_ref, yc_ref, wa_ref, wb_ref, wc_ref, g_ref, b_ref, o_ref, ob_ref):
    mix = (jnp.dot(ya_ref[...], wa_ref[...], preferred_element_type=F32)
           + jnp.dot(yb_ref[...], wb_ref[...], preferred_element_type=F32)
           + jnp.dot(yc_ref[...], wc_ref[...], preferred_element_type=F32))
    y = _layer_norm(ALPHA * x_ref[...] + mix, g_ref[...], b_ref[...])
    o_ref[...] = y
    ob_ref[...] = y.astype(BF16)


def mix_ln(x, ya_tb, yb, yc, w_out, g, b, bsz, seq):
    m, d = x.shape
    tm = MIX_TILE
    tiles = seq // tm
    row = lambda i: (i, 0)
    const = lambda i: (0, 0)
    return pl.pallas_call(
        _mix_kernel,
        grid=(m // tm,),
        in_specs=[pl.BlockSpec((tm, d), row),
                  pl.BlockSpec((tm, S5_WIDTH), lambda i: (i % tiles, i // tiles)),
                  pl.BlockSpec((tm, POOL_WIDTH), row),
                  pl.BlockSpec((tm, RET_WIDTH), row),
                  pl.BlockSpec((S5_WIDTH, d), lambda i: (0, 0)),
                  pl.BlockSpec((POOL_WIDTH, d), lambda i: (1, 0)),
                  pl.BlockSpec((RET_WIDTH, d), lambda i: (1, 0)),
                  pl.BlockSpec((1, d), const),
                  pl.BlockSpec((1, d), const)],
        out_specs=[pl.BlockSpec((tm, d), row), pl.BlockSpec((tm, d), row)],
        out_shape=[jax.ShapeDtypeStruct((m, d), F32), jax.ShapeDtypeStruct((m, d), BF16)],
        compiler_params=_params("parallel"),
        name="mix_ln",
    )(x, ya_tb, yb, yc, w_out, w_out, w_out, g, b)


def _resid_ln_kernel(x_ref, f_ref, g_ref, b_ref, o_ref, ob_ref):
    y = _layer_norm(ALPHA * x_ref[...] + f_ref[...], g_ref[...], b_ref[...])
    o_ref[...] = y
    ob_ref[...] = y.astype(BF16)


def resid_ln(x, f, g, b, tm=256):
    m, d = x.shape
    row = lambda i: (i, 0)
    const = lambda i: (0, 0)
    return pl.pallas_call(
        _resid_ln_kernel,
        grid=(m // tm,),
        in_specs=[pl.BlockSpec((tm, d), row), pl.BlockSpec((tm, d), row),
                  pl.BlockSpec((1, d), const), pl.BlockSpec((1, d), const)],
        out_specs=[pl.BlockSpec((tm, d), row), pl.BlockSpec((tm, d), row)],
        out_shape=[jax.ShapeDtypeStruct((m, d), F32), jax.ShapeDtypeStruct((m, d), BF16)],
        compiler_params=_params("parallel"),
        name="resid_ln",
    )(x, f, g, b)


PEER_SCORE_TILE = 256
NEG_BIG = -3.0e38
PEER_NCAND = PEER_TOPK + SUBLANES * (PEER_TOPK - 1)


def _top_desc(s, n, out_ref, with_rank):
    rank = jnp.full(s.shape, float(n), F32) if with_rank else None
    for r in range(n):
        m = jnp.max(s, axis=0, keepdims=True)
        out_ref[r:r + 1, :] = m
        hit = s >= m
        if with_rank:
            rank = jnp.where(hit, float(r), rank)
        if r < n - 1:
            s = jnp.where(hit, NEG_BIG, s)
    return rank


def _peer_scores_kernel(xb_ref, wq_ref, keys_ref, e1_ref, cnt_ref, e2_ref, r2_ref, a_scr, b_scr, cand_scr):
    q = jnp.dot(xb_ref[...], wq_ref[...], preferred_element_type=F32)
    nt = (((1,), (1,)), ((), ()))
    for h in range(PEER_HEADS):
        c0 = 2 * h * PEER_HALF
        s1_all = lax.dot_general(keys_ref[0], q[:, c0:c0 + PEER_HALF].astype(BF16), nt,
                                 preferred_element_type=F32)
        s2_all = lax.dot_general(keys_ref[1], q[:, c0 + PEER_HALF:c0 + 2 * PEER_HALF].astype(BF16), nt,
                                 preferred_element_type=F32)
        for lb in range(PEER_SCORE_TILE // LANES):
            cols = pl.ds(lb * LANES, LANES)
            s1 = s1_all[:, lb * LANES:(lb + 1) * LANES]
            s2 = s2_all[:, lb * LANES:(lb + 1) * LANES]
            s1 = s1 - jnp.max(s1, axis=0, keepdims=True)
            s2 = s2 - jnp.max(s2, axis=0, keepdims=True)
            _top_desc(s1, PEER_TOPK, a_scr, False)
            rank2 = _top_desc(s2, PEER_TOPK, b_scr, True)
            cand_scr[0:PEER_TOPK, :] = a_scr[0:1, :] + b_scr[...]
            for p in range(1, PEER_TOPK):
                r0 = PEER_TOPK + SUBLANES * (p - 1)
                cand_scr[r0:r0 + SUBLANES, :] = a_scr[p:p + 1, :] + b_scr[0:SUBLANES, :]
            cand = cand_scr[...]
            c = cand
            for _ in range(PEER_TOPK - 1):
                c = jnp.where(c >= jnp.max(c, axis=0, keepdims=True), NEG_BIG, c)
            tau = jnp.max(c, axis=0, keepdims=True)
            z = jnp.sum(jnp.where(cand >= tau, jnp.exp(cand), 0.0), axis=0, keepdims=True)
            b_top = b_scr[...]
            cnt = jnp.zeros(s1.shape, F32)
            for p in range(PEER_TOPK):
                a_p = a_scr[p:p + 1, :]
                c_p = jnp.sum(jnp.where(a_p + b_top >= tau, 1.0, 0.0), axis=0, keepdims=True)
                cnt = jnp.where(s1 == a_p, c_p, cnt)
            e1_ref[h, :, cols] = jnp.exp(s1) * (1.0 / z)
            cnt_ref[h, :, cols] = cnt
            e2_ref[h, :, cols] = jnp.exp(s2).astype(BF16)
            r2_ref[h, :, cols] = rank2.astype(BF16)


def peer_scores(xb, wq, keys):
    m, d = xb.shape
    t = PEER_SCORE_TILE
    sblk = pl.BlockSpec((PEER_HEADS, PEER_NKEYS, t), lambda i: (0, 0, i))
    sshape = (PEER_HEADS, PEER_NKEYS, m)
    return pl.pallas_call(
        _peer_scores_kernel,
        grid=(m // t,),
        in_specs=[pl.BlockSpec((t, d), lambda i: (i, 0)),
                  pl.BlockSpec((d, d), lambda i: (0, 0)),
                  pl.BlockSpec((2, PEER_NKEYS, PEER_HALF), lambda i: (0, 0, 0))],
        out_specs=[sblk, sblk, sblk, sblk],
        out_shape=[jax.ShapeDtypeStruct(sshape, F32), jax.ShapeDtypeStruct(sshape, F32),
                   jax.ShapeDtypeStruct(sshape, BF16), jax.ShapeDtypeStruct(sshape, BF16)],
        scratch_shapes=[pltpu.VMEM((PEER_TOPK, LANES), F32), pltpu.VMEM((PEER_TOPK, LANES), F32),
                        pltpu.VMEM((PEER_NCAND, LANES), F32)],
        compiler_params=_params("parallel"),
        name="peer_scores",
    )(xb, wq, keys)


PEER_TOKENS = 512
PEER_ETILE = 1024


PEER_FIRST_KEYS = PEER_ETILE // PEER_NKEYS
assert PEER_FIRST_KEYS == SUBLANES
BF16_ROWS = 2 * SUBLANES


FP8 = jnp.float8_e4m3fn
FP8_TARGET = 224.0


def _pow2_scale(amax):
    return jnp.exp2(jnp.floor(jnp.log2(FP8_TARGET / jnp.maximum(amax, 1e-30))))


def _peer_expert_kernel(sc_ref, xb_ref, e1_ref, cnt_ref, e2_in_ref, r2_in_ref, u_ref, vt_ref, o_ref,
                        acc_ref, at_ref, e2_ref, r2_ref, x8_ref):
    e = pl.program_id(1)
    x_scale = sc_ref[0:1, 0:1]
    inv_ux = sc_ref[0:1, 1:2]

    @pl.when(e == 0)
    def _():
        acc_ref[...] = jnp.zeros_like(acc_ref)
        e2_ref[...] = e2_in_ref[...]
        r2_ref[...] = r2_in_ref[...]
        x8_ref[...] = (xb_ref[...].astype(F32) * x_scale).astype(FP8)

    ht = lax.dot_general(u_ref[...], x8_ref[...], (((1,), (1,)), ((), ())), preferred_element_type=F32)
    gelu = (ht * (0.5 * inv_ux)) * (1.0 + lax.erf(ht * (inv_ux * (1.0 / math.sqrt(2.0)))))
    at_ref[...] = gelu.astype(BF16)
    first = pl.ds(pl.multiple_of(e * PEER_FIRST_KEYS, PEER_FIRST_KEYS), PEER_FIRST_KEYS)
    cnt_rows = [cnt_ref[h, first, :] for h in range(PEER_HEADS)]
    e1_rows = [e1_ref[h, first, :] for h in range(PEER_HEADS)]
    n_jb = PEER_NKEYS // BF16_ROWS
    for ii in range(PEER_FIRST_KEYS):
        gates = [jnp.zeros((BF16_ROWS, PEER_TOKENS), BF16) for _ in range(n_jb)]
        for h in range(PEER_HEADS):
            cnt_i = jnp.broadcast_to(cnt_rows[h][ii:ii + 1, :], (BF16_ROWS, PEER_TOKENS)).astype(BF16)
            e1_i = jnp.broadcast_to(e1_rows[h][ii:ii + 1, :], (BF16_ROWS, PEER_TOKENS)).astype(BF16)
            for jb in range(n_jb):
                js = pl.ds(jb * BF16_ROWS, BF16_ROWS)
                w = e2_ref[h, js, :] * e1_i
                gates[jb] = gates[jb] + jnp.where(r2_ref[h, js, :] < cnt_i, w, jnp.zeros_like(w))
        for jb in range(n_jb):
            rows = pl.ds(ii * PEER_NKEYS + jb * BF16_ROWS, BF16_ROWS)
            at_ref[rows, :] = at_ref[rows, :] * gates[jb]
    acc_ref[...] += jnp.dot(vt_ref[...], at_ref[...], preferred_element_type=F32)

    @pl.when(e == pl.num_programs(1) - 1)
    def _():
        o_ref[...] = acc_ref[...].T


def peer_experts(scales, xb, e1, cnt, e2, r2, u_tab, vt_tab):
    m, d = xb.shape
    t = PEER_TOKENS
    sblk = pl.BlockSpec((PEER_HEADS, PEER_NKEYS, t), lambda i, e: (0, 0, i))
    return pl.pallas_call(
        _peer_expert_kernel,
        grid=(m // t, PEER_EXPERTS // PEER_ETILE),
        in_specs=[pl.BlockSpec((1, LANES), lambda i, e: (0, 0)),
                  pl.BlockSpec((t, d), lambda i, e: (i, 0)),
                  sblk, sblk, sblk, sblk,
                  pl.BlockSpec((PEER_ETILE, d), lambda i, e: (e, 0)),
                  pl.BlockSpec((d, PEER_ETILE), lambda i, e: (0, e))],
        out_specs=pl.BlockSpec((t, d), lambda i, e: (i, 0)),
        out_shape=jax.ShapeDtypeStruct((m, d), F32),
        scratch_shapes=[pltpu.VMEM((d, t), F32),
                        pltpu.VMEM((PEER_ETILE, t), BF16),
                        pltpu.VMEM((PEER_HEADS, PEER_NKEYS, t), BF16),
                        pltpu.VMEM((PEER_HEADS, PEER_NKEYS, t), BF16),
                        pltpu.VMEM((t, d), FP8)],
        compiler_params=_params("parallel", "arbitrary"),
        name="peer_experts",
    )(scales, xb, e1, cnt, e2, r2, u_tab, vt_tab)


MM_TILE_M = 1024
MM_TILE_N = 512


def _trunk(x, bsz, seq, p):
    m = bsz * seq
    x, xb = layer_norm_rows(x, p['ln_in_g'], p['ln_in_b'])
    cos, sin = rotary_tables(seq)
    for l in range(DEPTH):
        w_in = p['w_in'][l]
        ua = matmul(xb, w_in, 0, S5_WIDTH, F32, MM_TILE_M, S5_WIDTH, time_major_batch=(bsz, seq), name="mm_in_s5")
        ua = ua.reshape(m, S5_WIDTH)
        ub = matmul(xb, w_in, S5_WIDTH, POOL_WIDTH, F32, MM_TILE_M, MM_TILE_N, name="mm_in_pool")
        hq = matmul(xb, w_in, S5_WIDTH + POOL_WIDTH, 4 * RET_WIDTH, BF16, MM_TILE_M, MM_TILE_N, name="mm_in_ret")
        y_fwd = s5_direction(ua, *p['s5_fwd'][l], bsz, False)
        ya = s5_direction(ua, *p['s5_bwd'][l], bsz, True,
                          fin=(y_fwd, p['s5_d'][l], p['s5_w_glu'][l], p['s5_b_glu'][l]))
        yb = pool_mixer(ub, p['pool_w'][l], p['pool_scale'][l], seq)
        part = retention_direction(hq, cos, sin, bsz, seq, False)
        yc = retention_direction(hq, cos, sin, bsz, seq, True, part=part)
        x, xb = mix_ln(x, ya.reshape(seq, bsz * S5_WIDTH), yb, yc, p['w_out'][l],
                       p['ln1_g'][l], p['ln1_b'][l], bsz, seq)
        e1, cnt, e2, r2 = peer_scores(xb, p['peer_w_q'][l], p['peer_keys'][l])
        ff = peer_experts(p['peer_scales'][l], xb, e1, cnt, e2, r2, p['peer_u'][l], p['peer_vt'][l])
        x, xb = resid_ln(x, ff, p['ln2_g'][l], p['ln2_b'][l])
    return x


def kernel(x_prompt, x_sample, ln_in_g, ln_in_b, w_in, s5_lambda_re, s5_lambda_im, s5_log_step, s5_b_re, s5_b_im, s5_c_re, s5_c_im, s5_d, s5_w_glu, s5_b_glu, pool_w, pool_scale, w_out, ln1_g, ln1_b, peer_w_q, peer_sub_keys, peer_u, peer_v, ln2_g, ln2_b):
    d = D_MODEL

    def s5_dir(direction):
        return [s5_weights(s5_lambda_re[l, direction], s5_lambda_im[l, direction], s5_log_step[l, direction],
                           s5_b_re[l, direction], s5_b_im[l, direction], s5_c_re[l, direction],
                           s5_c_im[l, direction]) for l in range(DEPTH)]

    u_scale = _pow2_scale(jnp.max(jnp.abs(peer_u), axis=(1, 2), keepdims=True))
    x_scale = _pow2_scale(math.sqrt(d) * jnp.max(jnp.abs(ln1_g), axis=1) + jnp.max(jnp.abs(ln1_b), axis=1))
    peer_scales = jnp.zeros((DEPTH, 1, LANES), F32)
    peer_scales = peer_scales.at[:, 0, 0].set(x_scale)
    peer_scales = peer_scales.at[:, 0, 1].set(1.0 / (u_scale[:, 0, 0] * x_scale))

    p = dict(
        ln_in_g=ln_in_g, ln_in_b=ln_in_b,
        w_in=w_in.astype(BF16),
        s5_fwd=s5_dir(0), s5_bwd=s5_dir(1),
        s5_d=s5_d.reshape(DEPTH, 1, S5_WIDTH), s5_w_glu=s5_w_glu.astype(BF16),
        s5_b_glu=s5_b_glu.reshape(DEPTH, 1, S5_WIDTH),
        pool_w=pool_w.astype(BF16), pool_scale=pool_scale.reshape(DEPTH, 1, POOL_WIDTH),
        w_out=w_out.astype(BF16),
        ln1_g=ln1_g.reshape(DEPTH, 1, d), ln1_b=ln1_b.reshape(DEPTH, 1, d),
        peer_w_q=peer_w_q.astype(BF16), peer_keys=peer_sub_keys.astype(BF16),
        peer_u=(peer_u * u_scale).astype(FP8), peer_vt=jnp.swapaxes(peer_v, 1, 2).astype(BF16),
        peer_scales=peer_scales,
        ln2_g=ln2_g.reshape(DEPTH, 1, d), ln2_b=ln2_b.reshape(DEPTH, 1, d),
    )
    outs = []
    for x in (x_prompt, x_sample):
        bsz, seq, _ = x.shape
        outs.append(_trunk(x.reshape(bsz * seq, d), bsz, seq, p).reshape(bsz, seq, d))
    return tuple(outs)
```

```python
import functools
import math

import jax
import jax.numpy as jnp
from jax import lax
from jax.experimental import pallas as pl
from jax.experimental.pallas import tpu as pltpu

F32 = jnp.float32
BF16 = jnp.bfloat16

D_MODEL = 2048
DEPTH = 4
S5_WIDTH = 512
POOL_WIDTH = 512
RET_WIDTH = 1024
S5_GROUP = 16
S5_GROUPS = 32
S5_STATE = 64
S5_NSTATE = S5_GROUPS * S5_STATE
POOL_WINDOWS = (2, 4, 8, 16)
POOL_GROUP = 128
RET_HEADS = 4
RET_HEAD_DIM = 256
RET_CHUNK = 128
PEER_HEADS = 8
PEER_HALF = 128
PEER_NKEYS = 128
PEER_EXPERTS = PEER_NKEYS * PEER_NKEYS
PEER_TOPK = 16
ALPHA = (2 * DEPTH) ** 0.25
LN_EPS = 1e-5
GN_EPS = 1e-6

LANES = 128
SUBLANES = 8
VMEM_LIMIT = 56 * 1024 * 1024


def _params(*sem, flags=None):
    return pltpu.CompilerParams(dimension_semantics=sem, vmem_limit_bytes=VMEM_LIMIT, flags=flags)


def _layer_norm(y, g, b):
    mu = jnp.mean(y, axis=-1, keepdims=True)
    yc = y - mu
    var = jnp.mean(yc * yc, axis=-1, keepdims=True)
    return yc * lax.rsqrt(var + LN_EPS) * g + b


def _gelu(x):
    return 0.5 * x * (1.0 + lax.erf(x * (1.0 / math.sqrt(2.0))))


def _ln_kernel(x_ref, g_ref, b_ref, y_ref, yb_ref):
    y = _layer_norm(x_ref[...], g_ref[...], b_ref[...])
    y_ref[...] = y
    yb_ref[...] = y.astype(BF16)


def layer_norm_rows(x, g, b, tm=256):
    m, d = x.shape
    return pl.pallas_call(
        _ln_kernel,
        grid=(m // tm,),
        in_specs=[pl.BlockSpec((tm, d), lambda i: (i, 0)),
                  pl.BlockSpec((1, d), lambda i: (0, 0)),
                  pl.BlockSpec((1, d), lambda i: (0, 0))],
        out_specs=[pl.BlockSpec((tm, d), lambda i: (i, 0)),
                   pl.BlockSpec((tm, d), lambda i: (i, 0))],
        out_shape=[jax.ShapeDtypeStruct((m, d), F32), jax.ShapeDtypeStruct((m, d), BF16)],
        compiler_params=_params("parallel"),
        name="ln_in",
    )(x, g.reshape(1, d), b.reshape(1, d))


def _mm_kernel(x_ref, w_ref, o_ref):
    o_ref[...] = jnp.dot(x_ref[...], w_ref[...], preferred_element_type=F32).astype(o_ref.dtype)


def matmul(xb, w, layer, col0, n, out_dtype, tm, tn, time_major_batch=None, name="mm"):
    m, k = xb.shape
    assert col0 % tn == 0 and n % tn == 0 and w.shape[1] == k
    j0 = col0 // tn
    if time_major_batch is None:
        out_shape = jax.ShapeDtypeStruct((m, n), out_dtype)
        out_spec = pl.BlockSpec((tm, tn), lambda i, j: (i, j))
    else:
        bsz, seq = time_major_batch
        assert n == tn and seq % tm == 0 and bsz * seq == m
        tiles = seq // tm
        out_shape = jax.ShapeDtypeStruct((seq, bsz * n), out_dtype)
        out_spec = pl.BlockSpec((tm, tn), lambda i, j: (i % tiles, i // tiles))
    return pl.pallas_call(
        _mm_kernel,
        grid=(m // tm, n // tn),
        in_specs=[pl.BlockSpec((tm, k), lambda i, j: (i, 0)),
                  pl.BlockSpec((None, k, tn), lambda i, j: (layer, 0, j0 + j))],
        out_specs=out_spec,
        out_shape=out_shape,
        compiler_params=_params("parallel", "arbitrary"),
        name=name,
    )(xb, w)


S5_ROWS = 512
S5_SLABS = 4
S5_SLAB_IN = S5_WIDTH // S5_SLABS
S5_COLS = S5_NSTATE // S5_SLABS


def _s5_kernel(*refs, bsz, reverse, finalize):
    if finalize:
        (u_ref, ar_ref, ai_ref, bslab_ref, cslab_ref, yprev_ref, d_ref, wglu_ref, bglu_ref,
         o_ref, xr_ref, xi_ref, bu_ref) = refs
    else:
        u_ref, ar_ref, ai_ref, bslab_ref, cslab_ref, o_ref, xr_ref, xi_ref, bu_ref = refs

    @pl.when(pl.program_id(0) == 0)
    def _():
        xr_ref[...] = jnp.zeros_like(xr_ref)
        xi_ref[...] = jnp.zeros_like(xi_ref)

    u = u_ref[...]
    ub = u.astype(BF16)
    steps = S5_ROWS // bsz
    ys = []
    for sl in range(S5_SLABS):
        cr = pl.ds(2 * sl * S5_COLS, S5_COLS)
        ci = pl.ds((2 * sl + 1) * S5_COLS, S5_COLS)
        both = pl.ds(2 * sl * S5_COLS, 2 * S5_COLS)
        st = pl.ds(sl * S5_COLS, S5_COLS)
        bu_ref[:, both] = jnp.dot(ub[:, sl * S5_SLAB_IN:(sl + 1) * S5_SLAB_IN], bslab_ref[sl],
                                  preferred_element_type=F32)
        ar = jnp.broadcast_to(ar_ref[:, st], (bsz, S5_COLS))
        ai = jnp.broadcast_to(ai_ref[:, st], (bsz, S5_COLS))

        def step(s, carry, cr=cr, ci=ci, ar=ar, ai=ai):
            xr, xi = carry
            t = (steps - 1 - s) if reverse else s
            rows = pl.ds(pl.multiple_of(t * bsz, bsz), bsz)
            nr = ar * xr - ai * xi + bu_ref[rows, cr]
            ni = ar * xi + ai * xr + bu_ref[rows, ci]
            bu_ref[rows, cr] = nr
            bu_ref[rows, ci] = ni
            return nr, ni

        xr, xi = lax.fori_loop(0, steps, step, (xr_ref[:, st], xi_ref[:, st]), unroll=8)
        xr_ref[:, st] = xr
        xi_ref[:, st] = xi
        ys.append(jnp.dot(bu_ref[:, both].astype(BF16), cslab_ref[sl], preferred_element_type=F32))
    y = jnp.concatenate(ys, axis=1)
    if finalize:
        y = _gelu(y + yprev_ref[...] + d_ref[...] * u)
        z = jnp.dot(y.astype(BF16), wglu_ref[...], preferred_element_type=F32) + bglu_ref[...]
        o_ref[...] = (y * jax.nn.sigmoid(z)).astype(o_ref.dtype)
    else:
        o_ref[...] = y


def s5_direction(u, abar_r, abar_i, bslab, cslab, bsz, reverse, fin=None):
    m = u.shape[0]
    nc = m // S5_ROWS
    cidx = (lambda c: (nc - 1 - c, 0)) if reverse else (lambda c: (c, 0))
    const = lambda c: (0, 0)
    in_specs = [pl.BlockSpec((S5_ROWS, S5_WIDTH), cidx),
                pl.BlockSpec((1, S5_NSTATE), const),
                pl.BlockSpec((1, S5_NSTATE), const),
                pl.BlockSpec((S5_SLABS, S5_SLAB_IN, 2 * S5_COLS), lambda c: (0, 0, 0)),
                pl.BlockSpec((S5_SLABS, 2 * S5_COLS, S5_SLAB_IN), lambda c: (0, 0, 0))]
    args = [u, abar_r, abar_i, bslab, cslab]
    if fin is not None:
        y_prev, d, w_glu, b_glu = fin
        in_specs += [pl.BlockSpec((S5_ROWS, S5_WIDTH), cidx),
                     pl.BlockSpec((1, S5_WIDTH), const),
                     pl.BlockSpec((S5_WIDTH, S5_WIDTH), const),
                     pl.BlockSpec((1, S5_WIDTH), const)]
        args += [y_prev, d, w_glu, b_glu]
    return pl.pallas_call(
        functools.partial(_s5_kernel, bsz=bsz, reverse=reverse, finalize=fin is not None),
        grid=(nc,),
        in_specs=in_specs,
        out_specs=pl.BlockSpec((S5_ROWS, S5_WIDTH), cidx),
        out_shape=jax.ShapeDtypeStruct((m, S5_WIDTH), BF16 if fin is not None else F32),
        scratch_shapes=[pltpu.VMEM((bsz, S5_NSTATE), F32),
                        pltpu.VMEM((bsz, S5_NSTATE), F32),
                        pltpu.VMEM((S5_ROWS, 2 * S5_NSTATE), F32)],
        compiler_params=_params("arbitrary"),
        name="s5_bwd" if reverse else "s5_fwd",
    )(*args)


def s5_weights(lam_re, lam_im, log_step, b_re, b_im, c_re, c_im):
    dt = jnp.exp(log_step)[:, None]
    mag = jnp.exp(lam_re * dt)
    abar_r = mag * jnp.cos(lam_im * dt)
    abar_i = mag * jnp.sin(lam_im * dt)
    den = lam_re * lam_re + lam_im * lam_im
    zr = ((abar_r - 1.0) * lam_re + abar_i * lam_im) / den
    zi = (abar_i * lam_re - (abar_r - 1.0) * lam_im) / den
    bb_r = zr[..., None] * b_re - zi[..., None] * b_im
    bb_i = zr[..., None] * b_im + zi[..., None] * b_re
    eye = jnp.eye(S5_GROUPS, dtype=F32)

    def b_block(bb):
        return jnp.einsum('gpc,gh->gchp', bb, eye).reshape(S5_WIDTH, S5_NSTATE)

    def c_block(cc):
        return jnp.einsum('gcp,gh->gphc', cc, eye).reshape(S5_NSTATE, S5_WIDTH)

    def slab(mat, sl, rows, cols):
        return mat[sl * rows:(sl + 1) * rows, sl * cols:(sl + 1) * cols]

    br, bi, cr, ci = b_block(bb_r), b_block(bb_i), c_block(c_re), c_block(c_im)
    bslab = jnp.stack([jnp.concatenate([slab(br, sl, S5_SLAB_IN, S5_COLS), slab(bi, sl, S5_SLAB_IN, S5_COLS)], axis=1)
                       for sl in range(S5_SLABS)]).astype(BF16)
    cslab = jnp.stack([jnp.concatenate([slab(cr, sl, S5_COLS, S5_SLAB_IN), -slab(ci, sl, S5_COLS, S5_SLAB_IN)], axis=0)
                       for sl in range(S5_SLABS)]).astype(BF16)
    return abar_r.reshape(1, S5_NSTATE), abar_i.reshape(1, S5_NSTATE), bslab, cslab


POOL_TILE = 256
POOL_HALO = 64


def _pool_kernel(prev_ref, u_ref, next_ref, w_ref, scale_ref, o_ref, *, tiles_per_seq, seq_len):
    t0 = (pl.program_id(0) % tiles_per_seq) * POOL_TILE
    u = u_ref[...]
    ext = jnp.concatenate([prev_ref[...], u, next_ref[...]], axis=0)
    n_ext = POOL_TILE + 2 * POOL_HALO
    t_out = t0 + lax.broadcasted_iota(jnp.int32, (POOL_TILE, 1), 0)
    t_src = t0 - POOL_HALO + lax.broadcasted_iota(jnp.int32, (1, n_ext), 1)
    outs = []
    for gi, w in enumerate(POOL_WINDOWS):
        lo = jnp.maximum(t_out - w // 2, 0)
        hi = jnp.minimum(t_out + w // 2 - 1, seq_len - 1)
        band = jnp.where((t_src >= lo) & (t_src <= hi), 1.0, 0.0).astype(BF16)
        cols = slice(gi * POOL_GROUP, (gi + 1) * POOL_GROUP)
        head = ext[:, cols].astype(BF16)
        rest = (ext[:, cols] - head.astype(F32)).astype(BF16)
        tot = (jnp.dot(band, head, preferred_element_type=F32) + jnp.dot(band, rest, preferred_element_type=F32))
        p = tot / (hi - lo + 1).astype(F32) - u[:, cols]
        outs.append(jnp.dot(p.astype(BF16), w_ref[gi], preferred_element_type=F32))
    o_ref[...] = (jnp.concatenate(outs, axis=1) * scale_ref[...]).astype(o_ref.dtype)


def pool_mixer(u, w, scale, seq_len):
    m = u.shape[0]
    tiles_per_seq = seq_len // POOL_TILE
    r = POOL_TILE // POOL_HALO
    last = m // POOL_HALO - 1
    return pl.pallas_call(
        functools.partial(_pool_kernel, tiles_per_seq=tiles_per_seq, seq_len=seq_len),
        grid=(m // POOL_TILE,),
        in_specs=[pl.BlockSpec((POOL_HALO, POOL_WIDTH), lambda i: (jnp.maximum(i * r - 1, 0), 0)),
                  pl.BlockSpec((POOL_TILE, POOL_WIDTH), lambda i: (i, 0)),
                  pl.BlockSpec((POOL_HALO, POOL_WIDTH), lambda i: (jnp.minimum((i + 1) * r, last), 0)),
                  pl.BlockSpec((len(POOL_WINDOWS), POOL_GROUP, POOL_GROUP), lambda i: (0, 0, 0)),
                  pl.BlockSpec((1, POOL_WIDTH), lambda i: (0, 0))],
        out_specs=pl.BlockSpec((POOL_TILE, POOL_WIDTH), lambda i: (i, 0)),
        out_shape=jax.ShapeDtypeStruct((m, POOL_WIDTH), BF16),
        compiler_params=_params("parallel"),
        name="pool",
    )(u, u, u, w, scale)


def _ret_kernel(*refs, reverse, finalize):
    if finalize:
        q_ref, k_ref, v_ref, cos_ref, sin_ref, g_ref, part_ref, o_ref, state_ref = refs
    else:
        q_ref, k_ref, v_ref, cos_ref, sin_ref, o_ref, state_ref = refs

    @pl.when(pl.program_id(1) == 0)
    def _():
        state_ref[...] = jnp.zeros_like(state_ref)

    cos = cos_ref[...]
    sin = sin_ref[...]
    d = RET_HEAD_DIM
    half = d // 2
    n = RET_CHUNK

    def rotary(x):
        x1, x2 = x[:, :half], x[:, half:]
        return jnp.concatenate([x1 * cos - x2 * sin, x2 * cos + x1 * sin], axis=1)

    pos = lax.broadcasted_iota(jnp.int32, (n, 1), 0).astype(F32)
    dist = jnp.abs(lax.broadcasted_iota(jnp.int32, (n, n), 0)
                   - lax.broadcasted_iota(jnp.int32, (n, n), 1)).astype(F32)
    for h in range(RET_HEADS):
        cs = slice(h * d, (h + 1) * d)
        log_gamma = math.log(1.0 - 2.0 ** (-5.0 - h))
        q = rotary(q_ref[:, cs].astype(F32))
        k = rotary(k_ref[:, cs].astype(F32)) * (d ** -0.5)
        vb = v_ref[:, cs]
        if reverse:
            q_decay = jnp.exp(log_gamma * (n - pos))
            k_decay = jnp.exp(log_gamma * pos)
        else:
            q_decay = jnp.exp(log_gamma * (pos + 1.0))
            k_decay = jnp.exp(log_gamma * (n - 1.0 - pos))
        state = state_ref[h]
        o = jnp.dot((q * q_decay).astype(BF16), state.astype(BF16), preferred_element_type=F32)
        if not reverse:
            scores = lax.dot_general(q.astype(BF16), k.astype(BF16), (((1,), (1,)), ((), ())),
                                     preferred_element_type=F32) * jnp.exp(log_gamma * dist)
            o = o + jnp.dot(scores.astype(BF16), vb, preferred_element_type=F32)
        kv = lax.dot_general((k * k_decay).astype(BF16), vb, (((0,), (0,)), ((), ())),
                             preferred_element_type=F32)
        state_ref[h] = math.exp(log_gamma * n) * state + kv
        if finalize:
            o = o + part_ref[:, cs]
            mu = jnp.mean(o, axis=-1, keepdims=True)
            oc = o - mu
            var = jnp.mean(oc * oc, axis=-1, keepdims=True)
            g = g_ref[:, cs].astype(F32)
            o_ref[:, cs] = (g * jax.nn.sigmoid(g) * (oc * lax.rsqrt(var + GN_EPS))).astype(o_ref.dtype)
        else:
            o_ref[:, cs] = o


def retention_direction(hq, cos, sin, bsz, seq, reverse, part=None):
    m = hq.shape[0]
    nc = seq // RET_CHUNK
    w = RET_WIDTH

    def rows(b, c):
        return b * nc + ((nc - 1 - c) if reverse else c)

    def col(j):
        return pl.BlockSpec((RET_CHUNK, w), lambda b, c, j=j: (rows(b, c), j))

    tab = pl.BlockSpec((RET_CHUNK, RET_HEAD_DIM // 2), lambda b, c: ((nc - 1 - c) if reverse else c, 0))
    in_specs = [col(0), col(1), col(2), tab, tab]
    args = [hq, hq, hq, cos, sin]
    if part is not None:
        in_specs += [col(3), col(0)]
        args += [hq, part]
    return pl.pallas_call(
        functools.partial(_ret_kernel, reverse=reverse, finalize=part is not None),
        grid=(bsz, nc),
        in_specs=in_specs,
        out_specs=col(0),
        out_shape=jax.ShapeDtypeStruct((m, w), BF16 if part is not None else F32),
        scratch_shapes=[pltpu.VMEM((RET_HEADS, RET_HEAD_DIM, RET_HEAD_DIM), F32)],
        compiler_params=_params("parallel", "arbitrary"),
        name="ret_bwd" if reverse else "ret_fwd",
    )(*args)


def rotary_tables(seq):
    half = RET_HEAD_DIM // 2
    inv = 1.0 / (10000.0 ** jnp.linspace(0.0, 1.0, half, dtype=F32))
    ang = jnp.arange(seq, dtype=F32)[:, None] * inv[None, :]
    return jnp.cos(ang), jnp.sin(ang)


MIX_TILE = 512


def _mix_kernel(sc_ref, x_ref, ya_ref, yb_ref, yc_ref, wa_ref, wb_ref, wc_ref, g_ref, b_ref, o_ref, ob_ref, o8_ref):
    mix = (jnp.dot(ya_ref[...], wa_ref[...], preferred_element_type=F32)
           + jnp.dot(yb_ref[...], wb_ref[...], preferred_element_type=F32)
           + jnp.dot(yc_ref[...], wc_ref[...], preferred_element_type=F32))
    y = _layer_norm(ALPHA * x_ref[...] + mix, g_ref[...], b_ref[...])
    o_ref[...] = y
    ob_ref[...] = y.astype(BF16)
    o8_ref[...] = (y * sc_ref[0:1, 0:1]).astype(FP8)


def mix_ln(scales, x, ya_tb, yb, yc, w_out, layer, g, b, bsz, seq):
    m, d = x.shape
    tm = MIX_TILE
    tiles = seq // tm
    row = lambda i: (i, 0)
    const = lambda i: (0, 0)
    return pl.pallas_call(
        _mix_kernel,
        grid=(m // tm,),
        in_specs=[pl.BlockSpec((None, 1, LANES), lambda i: (layer, 0, 0)),
                  pl.BlockSpec((tm, d), row),
                  pl.BlockSpec((tm, S5_WIDTH), lambda i: (i % tiles, i // tiles)),
                  pl.BlockSpec((tm, POOL_WIDTH), row),
                  pl.BlockSpec((tm, RET_WIDTH), row),
                  pl.BlockSpec((None, S5_WIDTH, d), lambda i: (layer, 0, 0)),
                  pl.BlockSpec((None, POOL_WIDTH, d), lambda i: (layer, 1, 0)),
                  pl.BlockSpec((None, RET_WIDTH, d), lambda i: (layer, 1, 0)),
                  pl.BlockSpec((1, d), const),
                  pl.BlockSpec((1, d), const)],
        out_specs=[pl.BlockSpec((tm, d), row), pl.BlockSpec((tm, d), row), pl.BlockSpec((tm, d), row)],
        out_shape=[jax.ShapeDtypeStruct((m, d), F32), jax.ShapeDtypeStruct((m, d), BF16),
                   jax.ShapeDtypeStruct((m, d), FP8)],
        compiler_params=_params("parallel"),
        name="mix_ln",
    )(scales, x, ya_tb, yb, yc, w_out, w_out, w_out, g, b)


def _resid_ln_kernel(x_ref, f_ref, g_ref, b_ref, o_ref, ob_ref):
    y = _layer_norm(ALPHA * x_ref[...] + f_ref[...], g_ref[...], b_ref[...])
    o_ref[...] = y
    ob_ref[...] = y.astype(BF16)


def resid_ln(x, f, g, b, tm=256):
    m, d = x.shape
    row = lambda i: (i, 0)
    const = lambda i: (0, 0)
    return pl.pallas_call(
        _resid_ln_kernel,
        grid=(m // tm,),
        in_specs=[pl.BlockSpec((tm, d), row), pl.BlockSpec((tm, d), row),
                  pl.BlockSpec((1, d), const), pl.BlockSpec((1, d), const)],
        out_specs=[pl.BlockSpec((tm, d), row), pl.BlockSpec((tm, d), row)],
        out_shape=[jax.ShapeDtypeStruct((m, d), F32), jax.ShapeDtypeStruct((m, d), BF16)],
        compiler_params=_params("parallel"),
        name="resid_ln",
    )(x, f, g, b)


PEER_SCORE_TILE = 256
NEG_BIG = -3.0e38
PEER_NCAND = PEER_TOPK + SUBLANES * (PEER_TOPK - 1)


def _top_desc(s, n, out_ref, with_rank):
    rank = jnp.full(s.shape, float(n), F32) if with_rank else None
    for r in range(n):
        m = jnp.max(s, axis=0, keepdims=True)
        out_ref[r:r + 1, :] = m
        hit = s >= m
        if with_rank:
            rank = jnp.where(hit, float(r), rank)
        if r < n - 1:
            s = jnp.where(hit, NEG_BIG, s)
    return rank


def _peer_scores_kernel(xb_ref, wq_ref, keys_ref, e1_ref, cnt_ref, e2_ref, r2_ref, a_scr, b_scr, cand_scr):
    q = jnp.dot(xb_ref[...], wq_ref[...], preferred_element_type=F32)
    nt = (((1,), (1,)), ((), ()))
    for h in range(PEER_HEADS):
        c0 = 2 * h * PEER_HALF
        s1_all = lax.dot_general(keys_ref[0], q[:, c0:c0 + PEER_HALF].astype(BF16), nt,
                                 preferred_element_type=F32)
        s2_all = lax.dot_general(keys_ref[1], q[:, c0 + PEER_HALF:c0 + 2 * PEER_HALF].astype(BF16), nt,
                                 preferred_element_type=F32)
        for lb in range(PEER_SCORE_TILE // LANES):
            cols = pl.ds(lb * LANES, LANES)
            s1 = s1_all[:, lb * LANES:(lb + 1) * LANES]
            s2 = s2_all[:, lb * LANES:(lb + 1) * LANES]
            s1 = s1 - jnp.max(s1, axis=0, keepdims=True)
            s2 = s2 - jnp.max(s2, axis=0, keepdims=True)
            _top_desc(s1, PEER_TOPK, a_scr, False)
            rank2 = _top_desc(s2, PEER_TOPK, b_scr, True)
            cand_scr[0:PEER_TOPK, :] = a_scr[0:1, :] + b_scr[...]
            for p in range(1, PEER_TOPK):
                r0 = PEER_TOPK + SUBLANES * (p - 1)
                cand_scr[r0:r0 + SUBLANES, :] = a_scr[p:p + 1, :] + b_scr[0:SUBLANES, :]
            cand = cand_scr[...]
            c = cand
            for _ in range(PEER_TOPK - 1):
                c = jnp.where(c >= jnp.max(c, axis=0, keepdims=True), NEG_BIG, c)
            tau = jnp.max(c, axis=0, keepdims=True)
            z = jnp.sum(jnp.where(cand >= tau, jnp.exp(cand), 0.0), axis=0, keepdims=True)
            b_top = b_scr[...]
            cnt = jnp.zeros(s1.shape, F32)
            for p in range(PEER_TOPK):
                a_p = a_scr[p:p + 1, :]
                c_p = jnp.sum(jnp.where(a_p + b_top >= tau, 1.0, 0.0), axis=0, keepdims=True)
                cnt = jnp.where(s1 == a_p, c_p, cnt)
            e1_ref[h, :, cols] = jnp.exp(s1) * (1.0 / z)
            cnt_ref[h, :, cols] = cnt
            e2_ref[h, :, cols] = jnp.exp(s2).astype(BF16)
            r2_ref[h, :, cols] = rank2.astype(BF16)


def peer_scores(xb, wq, keys, layer):
    m, d = xb.shape
    t = PEER_SCORE_TILE
    sblk = pl.BlockSpec((PEER_HEADS, PEER_NKEYS, t), lambda i: (0, 0, i))
    sshape = (PEER_HEADS, PEER_NKEYS, m)
    return pl.pallas_call(
        _peer_scores_kernel,
        grid=(m // t,),
        in_specs=[pl.BlockSpec((t, d), lambda i: (i, 0)),
                  pl.BlockSpec((None, d, d), lambda i: (layer, 0, 0)),
                  pl.BlockSpec((None, 2, PEER_NKEYS, PEER_HALF), lambda i: (layer, 0, 0, 0))],
        out_specs=[sblk, sblk, sblk, sblk],
        out_shape=[jax.ShapeDtypeStruct(sshape, F32), jax.ShapeDtypeStruct(sshape, F32),
                   jax.ShapeDtypeStruct(sshape, BF16), jax.ShapeDtypeStruct(sshape, BF16)],
        scratch_shapes=[pltpu.VMEM((PEER_TOPK, LANES), F32), pltpu.VMEM((PEER_TOPK, LANES), F32),
                        pltpu.VMEM((PEER_NCAND, LANES), F32)],
        compiler_params=_params("parallel"),
        name="peer_scores",
    )(xb, wq, keys)


PEER_TOKENS = 512
PEER_ETILE = 1024


PEER_FIRST_KEYS = PEER_ETILE // PEER_NKEYS
assert PEER_FIRST_KEYS == SUBLANES
BF16_ROWS = 2 * SUBLANES


FP8 = jnp.float8_e4m3fn
FP8_TARGET = 224.0


def _pow2_scale(amax):
    return jnp.exp2(jnp.floor(jnp.log2(FP8_TARGET / jnp.maximum(amax, 1e-30))))


def _peer_expert_kernel(sc_ref, x8_ref, e1_ref, cnt_ref, e2_in_ref, r2_in_ref, u_ref, vt_ref, o_ref,
                        acc_ref, at_ref, e2_ref, r2_ref, rows_ref, gate_ref):
    e = pl.program_id(1)
    inv_ux = sc_ref[0:1, 1:2]

    @pl.when(e == 0)
    def _():
        acc_ref[...] = jnp.zeros_like(acc_ref)
        e2_ref[...] = e2_in_ref[...]
        r2_ref[...] = r2_in_ref[...]

    first = pl.ds(pl.multiple_of(e * PEER_FIRST_KEYS, PEER_FIRST_KEYS), PEER_FIRST_KEYS)
    for h in range(PEER_HEADS):
        rows_ref[0, h] = cnt_ref[h, first, :]
        rows_ref[1, h] = e1_ref[h, first, :]
    n_jb = PEER_NKEYS // BF16_ROWS
    for ii in range(PEER_FIRST_KEYS):
        gates = [jnp.zeros((BF16_ROWS, PEER_TOKENS), BF16) for _ in range(n_jb)]
        for h in range(PEER_HEADS):
            cnt_i = jnp.broadcast_to(rows_ref[0, h, ii:ii + 1, :], (BF16_ROWS, PEER_TOKENS)).astype(BF16)
            e1_i = jnp.broadcast_to(rows_ref[1, h, ii:ii + 1, :], (BF16_ROWS, PEER_TOKENS)).astype(BF16)
            for jb in range(n_jb):
                js = pl.ds(jb * BF16_ROWS, BF16_ROWS)
                w = e2_ref[h, js, :] * e1_i
                gates[jb] = gates[jb] + jnp.where(r2_ref[h, js, :] < cnt_i, w, jnp.zeros_like(w))
        for jb in range(n_jb):
            gate_ref[pl.ds(ii * PEER_NKEYS + jb * BF16_ROWS, BF16_ROWS), :] = gates[jb]
    ht = lax.dot_general(u_ref[...], x8_ref[...], (((1,), (1,)), ((), ())), preferred_element_type=F32)
    gelu = (ht * (0.5 * inv_ux)) * (1.0 + lax.erf(ht * (inv_ux * (1.0 / math.sqrt(2.0)))))
    at_ref[...] = gelu.astype(BF16) * gate_ref[...]
    acc_ref[...] += jnp.dot(vt_ref[...], at_ref[...], preferred_element_type=F32)

    @pl.when(e == pl.num_programs(1) - 1)
    def _():
        o_ref[...] = acc_ref[...].T


def peer_experts(scales, x8, e1, cnt, e2, r2, u_tab, vt_tab, layer):
    m, d = x8.shape
    t = PEER_TOKENS
    sblk = pl.BlockSpec((PEER_HEADS, PEER_NKEYS, t), lambda i, e: (0, 0, i))
    return pl.pallas_call(
        _peer_expert_kernel,
        grid=(m // t, PEER_EXPERTS // PEER_ETILE),
        in_specs=[pl.BlockSpec((None, 1, LANES), lambda i, e: (layer, 0, 0)),
                  pl.BlockSpec((t, d), lambda i, e: (i, 0)),
                  sblk, sblk, sblk, sblk,
                  pl.BlockSpec((None, PEER_ETILE, d), lambda i, e: (layer, e, 0)),
                  pl.BlockSpec((None, d, PEER_ETILE), lambda i, e: (layer, 0, e))],
        out_specs=pl.BlockSpec((t, d), lambda i, e: (i, 0)),
        out_shape=jax.ShapeDtypeStruct((m, d), F32),
        scratch_shapes=[pltpu.VMEM((d, t), F32),
                        pltpu.VMEM((PEER_ETILE, t), BF16),
                        pltpu.VMEM((PEER_HEADS, PEER_NKEYS, t), BF16),
                        pltpu.VMEM((PEER_HEADS, PEER_NKEYS, t), BF16),
                        pltpu.VMEM((2, PEER_HEADS, PEER_FIRST_KEYS, t), F32),
                        pltpu.VMEM((PEER_ETILE, t), BF16)],
        compiler_params=_params("parallel", "arbitrary"),
        name="peer_experts",
    )(scales, x8, e1, cnt, e2, r2, u_tab, vt_tab)


MM_TILE_M = 1024
MM_TILE_N = 512


def _trunk(x, bsz, seq, p):
    m = bsz * seq
    x, xb = layer_norm_rows(x, p['ln_in_g'], p['ln_in_b'])
    cos, sin = rotary_tables(seq)
    for l in range(DEPTH):
        w_in = p['w_in']
        ua = matmul(xb, w_in, l, 0, S5_WIDTH, F32, MM_TILE_M, S5_WIDTH, time_major_batch=(bsz, seq), name="mm_in_s5")
        ua = ua.reshape(m, S5_WIDTH)
        ub = matmul(xb, w_in, l, S5_WIDTH, POOL_WIDTH, F32, MM_TILE_M, MM_TILE_N, name="mm_in_pool")
        hq = matmul(xb, w_in, l, S5_WIDTH + POOL_WIDTH, 4 * RET_WIDTH, BF16, MM_TILE_M, MM_TILE_N, name="mm_in_ret")
        y_fwd = s5_direction(ua, *p['s5_fwd'][l], bsz, False)
        ya = s5_direction(ua, *p['s5_bwd'][l], bsz, True,
                          fin=(y_fwd, p['s5_d'][l], p['s5_w_glu'][l], p['s5_b_glu'][l]))
        yb = pool_mixer(ub, p['pool_w'][l], p['pool_scale'][l], seq)
        part = retention_direction(hq, cos, sin, bsz, seq, False)
        yc = retention_direction(hq, cos, sin, bsz, seq, True, part=part)
        x, xb, x8 = mix_ln(p['peer_scales'], x, ya.reshape(seq, bsz * S5_WIDTH), yb, yc, p['w_out'], l,
                           p['ln1_g'][l], p['ln1_b'][l], bsz, seq)
        e1, cnt, e2, r2 = peer_scores(xb, p['peer_w_q'], p['peer_keys'], l)
        ff = peer_experts(p['peer_scales'], x8, e1, cnt, e2, r2, p['peer_u'], p['peer_vt'], l)
        x, xb = resid_ln(x, ff, p['ln2_g'][l], p['ln2_b'][l])
    return x


def kernel(x_prompt, x_sample, ln_in_g, ln_in_b, w_in, s5_lambda_re, s5_lambda_im, s5_log_step, s5_b_re, s5_b_im, s5_c_re, s5_c_im, s5_d, s5_w_glu, s5_b_glu, pool_w, pool_scale, w_out, ln1_g, ln1_b, peer_w_q, peer_sub_keys, peer_u, peer_v, ln2_g, ln2_b):
    d = D_MODEL

    def s5_dir(direction):
        return [s5_weights(s5_lambda_re[l, direction], s5_lambda_im[l, direction], s5_log_step[l, direction],
                           s5_b_re[l, direction], s5_b_im[l, direction], s5_c_re[l, direction],
                           s5_c_im[l, direction]) for l in range(DEPTH)]

    u_scale = _pow2_scale(jnp.max(jnp.abs(peer_u), axis=(1, 2), keepdims=True))
    x_scale = _pow2_scale(math.sqrt(d) * jnp.max(jnp.abs(ln1_g), axis=1) + jnp.max(jnp.abs(ln1_b), axis=1))
    peer_scales = jnp.zeros((DEPTH, 1, LANES), F32)
    peer_scales = peer_scales.at[:, 0, 0].set(x_scale)
    peer_scales = peer_scales.at[:, 0, 1].set(1.0 / (u_scale[:, 0, 0] * x_scale))

    p = dict(
        ln_in_g=ln_in_g, ln_in_b=ln_in_b,
        w_in=w_in.astype(BF16),
        s5_fwd=s5_dir(0), s5_bwd=s5_dir(1),
        s5_d=s5_d.reshape(DEPTH, 1, S5_WIDTH), s5_w_glu=s5_w_glu.astype(BF16),
        s5_b_glu=s5_b_glu.reshape(DEPTH, 1, S5_WIDTH),
        pool_w=pool_w.astype(BF16), pool_scale=pool_scale.reshape(DEPTH, 1, POOL_WIDTH),
        w_out=w_out.astype(BF16),
        ln1_g=ln1_g.reshape(DEPTH, 1, d), ln1_b=ln1_b.reshape(DEPTH, 1, d),
        peer_w_q=peer_w_q.astype(BF16), peer_keys=peer_sub_keys.astype(BF16),
        peer_u=(peer_u * u_scale).astype(FP8), peer_vt=jnp.swapaxes(peer_v, 1, 2).astype(BF16),
        peer_scales=peer_scales,
        ln2_g=ln2_g.reshape(DEPTH, 1, d), ln2_b=ln2_b.reshape(DEPTH, 1, d),
    )
    outs = []
    for x in (x_prompt, x_sample):
        bsz, seq, _ = x.shape
        outs.append(_trunk(x.reshape(bsz * seq, d), bsz, seq, p).reshape(bsz, seq, d))
    return tuple(outs)
```

```python
import functools
import math

import jax
import jax.numpy as jnp
from jax import lax
from jax.experimental import pallas as pl
from jax.experimental.pallas import tpu as pltpu

F32 = jnp.float32
BF16 = jnp.bfloat16

D_MODEL = 2048
DEPTH = 4
S5_WIDTH = 512
POOL_WIDTH = 512
RET_WIDTH = 1024
S5_GROUP = 16
S5_GROUPS = 32
S5_STATE = 64
S5_NSTATE = S5_GROUPS * S5_STATE
POOL_WINDOWS = (2, 4, 8, 16)
POOL_GROUP = 128
RET_HEADS = 4
RET_HEAD_DIM = 256
RET_CHUNK = 128
PEER_HEADS = 8
PEER_HALF = 128
PEER_NKEYS = 128
PEER_EXPERTS = PEER_NKEYS * PEER_NKEYS
PEER_TOPK = 16
ALPHA = (2 * DEPTH) ** 0.25
LN_EPS = 1e-5
GN_EPS = 1e-6

LANES = 128
SUBLANES = 8
VMEM_LIMIT = 56 * 1024 * 1024


def _params(*sem, flags=None):
    return pltpu.CompilerParams(dimension_semantics=sem, vmem_limit_bytes=VMEM_LIMIT, flags=flags)


def _layer_norm(y, g, b):
    mu = jnp.mean(y, axis=-1, keepdims=True)
    yc = y - mu
    var = jnp.mean(yc * yc, axis=-1, keepdims=True)
    return yc * lax.rsqrt(var + LN_EPS) * g + b


def _gelu(x):
    return 0.5 * x * (1.0 + lax.erf(x * (1.0 / math.sqrt(2.0))))


def _ln_kernel(x_ref, g_ref, b_ref, y_ref, yb_ref):
    y = _layer_norm(x_ref[...], g_ref[...], b_ref[...])
    y_ref[...] = y
    yb_ref[...] = y.astype(BF16)


def layer_norm_rows(x, g, b, tm=256):
    m, d = x.shape
    return pl.pallas_call(
        _ln_kernel,
        grid=(m // tm,),
        in_specs=[pl.BlockSpec((tm, d), lambda i: (i, 0)),
                  pl.BlockSpec((1, d), lambda i: (0, 0)),
                  pl.BlockSpec((1, d), lambda i: (0, 0))],
        out_specs=[pl.BlockSpec((tm, d), lambda i: (i, 0)),
                   pl.BlockSpec((tm, d), lambda i: (i, 0))],
        out_shape=[jax.ShapeDtypeStruct((m, d), F32), jax.ShapeDtypeStruct((m, d), BF16)],
        compiler_params=_params("parallel"),
        name="ln_in",
    )(x, g.reshape(1, d), b.reshape(1, d))


def _mm_kernel(x_ref, w_ref, o_ref):
    o_ref[...] = jnp.dot(x_ref[...], w_ref[...], preferred_element_type=F32).astype(o_ref.dtype)


def matmul(xb, w, layer, col0, n, out_dtype, tm, tn, time_major_batch=None, name="mm"):
    m, k = xb.shape
    assert col0 % tn == 0 and n % tn == 0 and w.shape[1] == k
    j0 = col0 // tn
    if time_major_batch is None:
        out_shape = jax.ShapeDtypeStruct((m, n), out_dtype)
        out_spec = pl.BlockSpec((tm, tn), lambda i, j: (i, j))
    else:
        bsz, seq = time_major_batch
        assert n == tn and seq % tm == 0 and bsz * seq == m
        tiles = seq // tm
        out_shape = jax.ShapeDtypeStruct((seq, bsz * n), out_dtype)
        out_spec = pl.BlockSpec((tm, tn), lambda i, j: (i % tiles, i // tiles))
    return pl.pallas_call(
        _mm_kernel,
        grid=(m // tm, n // tn),
        in_specs=[pl.BlockSpec((tm, k), lambda i, j: (i, 0)),
                  pl.BlockSpec((None, k, tn), lambda i, j: (layer, 0, j0 + j))],
        out_specs=out_spec,
        out_shape=out_shape,
        compiler_params=_params("parallel", "arbitrary"),
        name=name,
    )(xb, w)


S5_ROWS = 512
S5_SLABS = 4
S5_SLAB_IN = S5_WIDTH // S5_SLABS
S5_COLS = S5_NSTATE // S5_SLABS


def _s5_kernel(*refs, bsz, reverse, finalize):
    if finalize:
        (u_ref, ar_ref, ai_ref, bslab_ref, cslab_ref, yprev_ref, d_ref, wglu_ref, bglu_ref,
         o_ref, xr_ref, xi_ref, bu_ref) = refs
    else:
        u_ref, ar_ref, ai_ref, bslab_ref, cslab_ref, o_ref, xr_ref, xi_ref, bu_ref = refs

    @pl.when(pl.program_id(0) == 0)
    def _():
        xr_ref[...] = jnp.zeros_like(xr_ref)
        xi_ref[...] = jnp.zeros_like(xi_ref)

    u = u_ref[...]
    ub = u.astype(BF16)
    steps = S5_ROWS // bsz
    ys = []
    for sl in range(S5_SLABS):
        cr = pl.ds(2 * sl * S5_COLS, S5_COLS)
        ci = pl.ds((2 * sl + 1) * S5_COLS, S5_COLS)
        both = pl.ds(2 * sl * S5_COLS, 2 * S5_COLS)
        st = pl.ds(sl * S5_COLS, S5_COLS)
        bu_ref[:, both] = jnp.dot(ub[:, sl * S5_SLAB_IN:(sl + 1) * S5_SLAB_IN], bslab_ref[sl],
                                  preferred_element_type=F32)
        ar = jnp.broadcast_to(ar_ref[:, st], (bsz, S5_COLS))
        ai = jnp.broadcast_to(ai_ref[:, st], (bsz, S5_COLS))

        def step(s, carry, cr=cr, ci=ci, ar=ar, ai=ai):
            xr, xi = carry
            t = (steps - 1 - s) if reverse else s
            rows = pl.ds(pl.multiple_of(t * bsz, bsz), bsz)
            nr = ar * xr - ai * xi + bu_ref[rows, cr]
            ni = ar * xi + ai * xr + bu_ref[rows, ci]
            bu_ref[rows, cr] = nr
            bu_ref[rows, ci] = ni
            return nr, ni

        xr, xi = lax.fori_loop(0, steps, step, (xr_ref[:, st], xi_ref[:, st]), unroll=8)
        xr_ref[:, st] = xr
        xi_ref[:, st] = xi
        ys.append(jnp.dot(bu_ref[:, both].astype(BF16), cslab_ref[sl], preferred_element_type=F32))
    y = jnp.concatenate(ys, axis=1)
    if finalize:
        y = _gelu(y + yprev_ref[...] + d_ref[...] * u)
        z = jnp.dot(y.astype(BF16), wglu_ref[...], preferred_element_type=F32) + bglu_ref[...]
        o_ref[...] = (y * jax.nn.sigmoid(z)).astype(o_ref.dtype)
    else:
        o_ref[...] = y


def s5_direction(u, abar_r, abar_i, bslab, cslab, bsz, reverse, fin=None):
    m = u.shape[0]
    nc = m // S5_ROWS
    cidx = (lambda c: (nc - 1 - c, 0)) if reverse else (lambda c: (c, 0))
    const = lambda c: (0, 0)
    in_specs = [pl.BlockSpec((S5_ROWS, S5_WIDTH), cidx),
                pl.BlockSpec((1, S5_NSTATE), const),
                pl.BlockSpec((1, S5_NSTATE), const),
                pl.BlockSpec((S5_SLABS, S5_SLAB_IN, 2 * S5_COLS), lambda c: (0, 0, 0)),
                pl.BlockSpec((S5_SLABS, 2 * S5_COLS, S5_SLAB_IN), lambda c: (0, 0, 0))]
    args = [u, abar_r, abar_i, bslab, cslab]
    if fin is not None:
        y_prev, d, w_glu, b_glu = fin
        in_specs += [pl.BlockSpec((S5_ROWS, S5_WIDTH), cidx),
                     pl.BlockSpec((1, S5_WIDTH), const),
                     pl.BlockSpec((S5_WIDTH, S5_WIDTH), const),
                     pl.BlockSpec((1, S5_WIDTH), const)]
        args += [y_prev, d, w_glu, b_glu]
    return pl.pallas_call(
        functools.partial(_s5_kernel, bsz=bsz, reverse=reverse, finalize=fin is not None),
        grid=(nc,),
        in_specs=in_specs,
        out_specs=pl.BlockSpec((S5_ROWS, S5_WIDTH), cidx),
        out_shape=jax.ShapeDtypeStruct((m, S5_WIDTH), BF16 if fin is not None else F32),
        scratch_shapes=[pltpu.VMEM((bsz, S5_NSTATE), F32),
                        pltpu.VMEM((bsz, S5_NSTATE), F32),
                        pltpu.VMEM((S5_ROWS, 2 * S5_NSTATE), F32)],
        compiler_params=_params("arbitrary"),
        name="s5_bwd" if reverse else "s5_fwd",
    )(*args)


def s5_weights(lam_re, lam_im, log_step, b_re, b_im, c_re, c_im):
    dt = jnp.exp(log_step)[:, None]
    mag = jnp.exp(lam_re * dt)
    abar_r = mag * jnp.cos(lam_im * dt)
    abar_i = mag * jnp.sin(lam_im * dt)
    den = lam_re * lam_re + lam_im * lam_im
    zr = ((abar_r - 1.0) * lam_re + abar_i * lam_im) / den
    zi = (abar_i * lam_re - (abar_r - 1.0) * lam_im) / den
    bb_r = zr[..., None] * b_re - zi[..., None] * b_im
    bb_i = zr[..., None] * b_im + zi[..., None] * b_re
    eye = jnp.eye(S5_GROUPS, dtype=F32)

    def b_block(bb):
        return jnp.einsum('gpc,gh->gchp', bb, eye).reshape(S5_WIDTH, S5_NSTATE)

    def c_block(cc):
        return jnp.einsum('gcp,gh->gphc', cc, eye).reshape(S5_NSTATE, S5_WIDTH)

    def slab(mat, sl, rows, cols):
        return mat[sl * rows:(sl + 1) * rows, sl * cols:(sl + 1) * cols]

    br, bi, cr, ci = b_block(bb_r), b_block(bb_i), c_block(c_re), c_block(c_im)
    bslab = jnp.stack([jnp.concatenate([slab(br, sl, S5_SLAB_IN, S5_COLS), slab(bi, sl, S5_SLAB_IN, S5_COLS)], axis=1)
                       for sl in range(S5_SLABS)]).astype(BF16)
    cslab = jnp.stack([jnp.concatenate([slab(cr, sl, S5_COLS, S5_SLAB_IN), -slab(ci, sl, S5_COLS, S5_SLAB_IN)], axis=0)
                       for sl in range(S5_SLABS)]).astype(BF16)
    return abar_r.reshape(1, S5_NSTATE), abar_i.reshape(1, S5_NSTATE), bslab, cslab


POOL_TILE = 256
POOL_HALO = 64


def _pool_kernel(prev_ref, u_ref, next_ref, w_ref, scale_ref, o_ref, *, tiles_per_seq, seq_len):
    t0 = (pl.program_id(0) % tiles_per_seq) * POOL_TILE
    u = u_ref[...]
    ext = jnp.concatenate([prev_ref[...], u, next_ref[...]], axis=0)
    n_ext = POOL_TILE + 2 * POOL_HALO
    t_out = t0 + lax.broadcasted_iota(jnp.int32, (POOL_TILE, 1), 0)
    t_src = t0 - POOL_HALO + lax.broadcasted_iota(jnp.int32, (1, n_ext), 1)
    outs = []
    for gi, w in enumerate(POOL_WINDOWS):
        lo = jnp.maximum(t_out - w // 2, 0)
        hi = jnp.minimum(t_out + w // 2 - 1, seq_len - 1)
        band = jnp.where((t_src >= lo) & (t_src <= hi), 1.0, 0.0).astype(BF16)
        cols = slice(gi * POOL_GROUP, (gi + 1) * POOL_GROUP)
        head = ext[:, cols].astype(BF16)
        rest = (ext[:, cols] - head.astype(F32)).astype(BF16)
        tot = (jnp.dot(band, head, preferred_element_type=F32) + jnp.dot(band, rest, preferred_element_type=F32))
        p = tot / (hi - lo + 1).astype(F32) - u[:, cols]
        outs.append(jnp.dot(p.astype(BF16), w_ref[gi], preferred_element_type=F32))
    o_ref[...] = (jnp.concatenate(outs, axis=1) * scale_ref[...]).astype(o_ref.dtype)


def pool_mixer(u, w, scale, seq_len):
    m = u.shape[0]
    tiles_per_seq = seq_len // POOL_TILE
    r = POOL_TILE // POOL_HALO
    last = m // POOL_HALO - 1
    return pl.pallas_call(
        functools.partial(_pool_kernel, tiles_per_seq=tiles_per_seq, seq_len=seq_len),
        grid=(m // POOL_TILE,),
        in_specs=[pl.BlockSpec((POOL_HALO, POOL_WIDTH), lambda i: (jnp.maximum(i * r - 1, 0), 0)),
                  pl.BlockSpec((POOL_TILE, POOL_WIDTH), lambda i: (i, 0)),
                  pl.BlockSpec((POOL_HALO, POOL_WIDTH), lambda i: (jnp.minimum((i + 1) * r, last), 0)),
                  pl.BlockSpec((len(POOL_WINDOWS), POOL_GROUP, POOL_GROUP), lambda i: (0, 0, 0)),
                  pl.BlockSpec((1, POOL_WIDTH), lambda i: (0, 0))],
        out_specs=pl.BlockSpec((POOL_TILE, POOL_WIDTH), lambda i: (i, 0)),
        out_shape=jax.ShapeDtypeStruct((m, POOL_WIDTH), BF16),
        compiler_params=_params("parallel"),
        name="pool",
    )(u, u, u, w, scale)


def _ret_kernel(*refs, reverse, finalize):
    if finalize:
        q_ref, k_ref, v_ref, cos_ref, sin_ref, g_ref, part_ref, o_ref, state_ref = refs
    else:
        q_ref, k_ref, v_ref, cos_ref, sin_ref, o_ref, state_ref = refs

    @pl.when(pl.program_id(1) == 0)
    def _():
        state_ref[...] = jnp.zeros_like(state_ref)

    cos = cos_ref[...]
    sin = sin_ref[...]
    d = RET_HEAD_DIM
    half = d // 2
    n = RET_CHUNK

    def rotary(x):
        x1, x2 = x[:, :half], x[:, half:]
        return jnp.concatenate([x1 * cos - x2 * sin, x2 * cos + x1 * sin], axis=1)

    pos = lax.broadcasted_iota(jnp.int32, (n, 1), 0).astype(F32)
    dist = jnp.abs(lax.broadcasted_iota(jnp.int32, (n, n), 0)
                   - lax.broadcasted_iota(jnp.int32, (n, n), 1)).astype(F32)
    for h in range(RET_HEADS):
        cs = slice(h * d, (h + 1) * d)
        log_gamma = math.log(1.0 - 2.0 ** (-5.0 - h))
        q = rotary(q_ref[:, cs].astype(F32))
        k = rotary(k_ref[:, cs].astype(F32)) * (d ** -0.5)
        vb = v_ref[:, cs]
        if reverse:
            q_decay = jnp.exp(log_gamma * (n - pos))
            k_decay = jnp.exp(log_gamma * pos)
        else:
            q_decay = jnp.exp(log_gamma * (pos + 1.0))
            k_decay = jnp.exp(log_gamma * (n - 1.0 - pos))
        state = state_ref[h]
        o = jnp.dot((q * q_decay).astype(BF16), state.astype(BF16), preferred_element_type=F32)
        if not reverse:
            scores = lax.dot_general(q.astype(BF16), k.astype(BF16), (((1,), (1,)), ((), ())),
                                     preferred_element_type=F32) * jnp.exp(log_gamma * dist)
            o = o + jnp.dot(scores.astype(BF16), vb, preferred_element_type=F32)
        kv = lax.dot_general((k * k_decay).astype(BF16), vb, (((0,), (0,)), ((), ())),
                             preferred_element_type=F32)
        state_ref[h] = math.exp(log_gamma * n) * state + kv
        if finalize:
            o = o + part_ref[:, cs]
            mu = jnp.mean(o, axis=-1, keepdims=True)
            oc = o - mu
            var = jnp.mean(oc * oc, axis=-1, keepdims=True)
            g = g_ref[:, cs].astype(F32)
            o_ref[:, cs] = (g * jax.nn.sigmoid(g) * (oc * lax.rsqrt(var + GN_EPS))).astype(o_ref.dtype)
        else:
            o_ref[:, cs] = o


def retention_direction(hq, cos, sin, bsz, seq, reverse, part=None):
    m = hq.shape[0]
    nc = seq // RET_CHUNK
    w = RET_WIDTH

    def rows(b, c):
        return b * nc + ((nc - 1 - c) if reverse else c)

    def col(j):
        return pl.BlockSpec((RET_CHUNK, w), lambda b, c, j=j: (rows(b, c), j))

    tab = pl.BlockSpec((RET_CHUNK, RET_HEAD_DIM // 2), lambda b, c: ((nc - 1 - c) if reverse else c, 0))
    in_specs = [col(0), col(1), col(2), tab, tab]
    args = [hq, hq, hq, cos, sin]
    if part is not None:
        in_specs += [col(3), col(0)]
        args += [hq, part]
    return pl.pallas_call(
        functools.partial(_ret_kernel, reverse=reverse, finalize=part is not None),
        grid=(bsz, nc),
        in_specs=in_specs,
        out_specs=col(0),
        out_shape=jax.ShapeDtypeStruct((m, w), BF16 if part is not None else F32),
        scratch_shapes=[pltpu.VMEM((RET_HEADS, RET_HEAD_DIM, RET_HEAD_DIM), F32)],
        compiler_params=_params("parallel", "arbitrary"),
        name="ret_bwd" if reverse else "ret_fwd",
    )(*args)


def rotary_tables(seq):
    half = RET_HEAD_DIM // 2
    inv = 1.0 / (10000.0 ** jnp.linspace(0.0, 1.0, half, dtype=F32))
    ang = jnp.arange(seq, dtype=F32)[:, None] * inv[None, :]
    return jnp.cos(ang), jnp.sin(ang)


MIX_TILE = 512


def _mix_kernel(sc_ref, x_ref, ya_ref, yb_ref, yc_ref, wa_ref, wb_ref, wc_ref, g_ref, b_ref, o_ref, ob_ref, o8_ref):
    mix = (jnp.dot(ya_ref[...], wa_ref[...], preferred_element_type=F32)
           + jnp.dot(yb_ref[...], wb_ref[...], preferred_element_type=F32)
           + jnp.dot(yc_ref[...], wc_ref[...], preferred_element_type=F32))
    y = _layer_norm(ALPHA * x_ref[...] + mix, g_ref[...], b_ref[...])
    o_ref[...] = y
    ob_ref[...] = y.astype(BF16)
    o8_ref[...] = (y * sc_ref[0:1, 0:1]).astype(FP8)


def mix_ln(scales, x, ya_tb, yb, yc, w_out, layer, g, b, bsz, seq):
    m, d = x.shape
    tm = MIX_TILE
    tiles = seq // tm
    row = lambda i: (i, 0)
    const = lambda i: (0, 0)
    return pl.pallas_call(
        _mix_kernel,
        grid=(m // tm,),
        in_specs=[pl.BlockSpec((None, 1, LANES), lambda i: (layer, 0, 0)),
                  pl.BlockSpec((tm, d), row),
                  pl.BlockSpec((tm, S5_WIDTH), lambda i: (i % tiles, i // tiles)),
                  pl.BlockSpec((tm, POOL_WIDTH), row),
                  pl.BlockSpec((tm, RET_WIDTH), row),
                  pl.BlockSpec((None, S5_WIDTH, d), lambda i: (layer, 0, 0)),
                  pl.BlockSpec((None, POOL_WIDTH, d), lambda i: (layer, 1, 0)),
                  pl.BlockSpec((None, RET_WIDTH, d), lambda i: (layer, 1, 0)),
                  pl.BlockSpec((1, d), const),
                  pl.BlockSpec((1, d), const)],
        out_specs=[pl.BlockSpec((tm, d), row), pl.BlockSpec((tm, d), row), pl.BlockSpec((tm, d), row)],
        out_shape=[jax.ShapeDtypeStruct((m, d), F32), jax.ShapeDtypeStruct((m, d), BF16),
                   jax.ShapeDtypeStruct((m, d), FP8)],
        compiler_params=_params("parallel"),
        name="mix_ln",
    )(scales, x, ya_tb, yb, yc, w_out, w_out, w_out, g, b)


def _resid_ln_kernel(x_ref, f_ref, g_ref, b_ref, o_ref, ob_ref):
    y = _layer_norm(ALPHA * x_ref[...] + f_ref[...], g_ref[...], b_ref[...])
    o_ref[...] = y
    ob_ref[...] = y.astype(BF16)


def resid_ln(x, f, g, b, tm=256):
    m, d = x.shape
    row = lambda i: (i, 0)
    const = lambda i: (0, 0)
    return pl.pallas_call(
        _resid_ln_kernel,
        grid=(m // tm,),
        in_specs=[pl.BlockSpec((tm, d), row), pl.BlockSpec((tm, d), row),
                  pl.BlockSpec((1, d), const), pl.BlockSpec((1, d), const)],
        out_specs=[pl.BlockSpec((tm, d), row), pl.BlockSpec((tm, d), row)],
        out_shape=[jax.ShapeDtypeStruct((m, d), F32), jax.ShapeDtypeStruct((m, d), BF16)],
        compiler_params=_params("parallel"),
        name="resid_ln",
    )(x, f, g, b)


PEER_SCORE_TILE = 512
NEG_BIG = -3.0e38
PEER_NCAND = PEER_TOPK + SUBLANES * (PEER_TOPK - 1)


def _top_desc(s, n, out_ref, with_rank):
    rank = jnp.full(s.shape, float(n), F32) if with_rank else None
    for r in range(n):
        m = jnp.max(s, axis=0, keepdims=True)
        out_ref[r:r + 1, :] = m
        hit = s >= m
        if with_rank:
            rank = jnp.where(hit, float(r), rank)
        if r < n - 1:
            s = jnp.where(hit, NEG_BIG, s)
    return rank


def _peer_scores_kernel(xb_ref, wq_ref, keys_ref, e1_ref, cnt_ref, e2_ref, r2_ref, a_scr, b_scr, cand_scr):
    q = jnp.dot(xb_ref[...], wq_ref[...], preferred_element_type=F32)
    nt = (((1,), (1,)), ((), ()))
    for h in range(PEER_HEADS):
        c0 = 2 * h * PEER_HALF
        s1_all = lax.dot_general(keys_ref[0], q[:, c0:c0 + PEER_HALF].astype(BF16), nt,
                                 preferred_element_type=F32)
        s2_all = lax.dot_general(keys_ref[1], q[:, c0 + PEER_HALF:c0 + 2 * PEER_HALF].astype(BF16), nt,
                                 preferred_element_type=F32)
        for lb in range(PEER_SCORE_TILE // LANES):
            cols = pl.ds(lb * LANES, LANES)
            s1 = s1_all[:, lb * LANES:(lb + 1) * LANES]
            s2 = s2_all[:, lb * LANES:(lb + 1) * LANES]
            s1 = s1 - jnp.max(s1, axis=0, keepdims=True)
            s2 = s2 - jnp.max(s2, axis=0, keepdims=True)
            _top_desc(s1, PEER_TOPK, a_scr, False)
            rank2 = _top_desc(s2, PEER_TOPK, b_scr, True)
            cand_scr[0:PEER_TOPK, :] = a_scr[0:1, :] + b_scr[...]
            for p in range(1, PEER_TOPK):
                r0 = PEER_TOPK + SUBLANES * (p - 1)
                cand_scr[r0:r0 + SUBLANES, :] = a_scr[p:p + 1, :] + b_scr[0:SUBLANES, :]
            cand = cand_scr[...]
            c = cand
            for _ in range(PEER_TOPK - 1):
                c = jnp.where(c >= jnp.max(c, axis=0, keepdims=True), NEG_BIG, c)
            tau = jnp.max(c, axis=0, keepdims=True)
            z = jnp.sum(jnp.where(cand >= tau, jnp.exp(cand), 0.0), axis=0, keepdims=True)
            b_top = b_scr[...]
            cnt = jnp.zeros(s1.shape, F32)
            for p in range(PEER_TOPK):
                a_p = a_scr[p:p + 1, :]
                c_p = jnp.sum(jnp.where(a_p + b_top >= tau, 1.0, 0.0), axis=0, keepdims=True)
                cnt = jnp.where(s1 == a_p, c_p, cnt)
            e1_ref[h, :, cols] = jnp.exp(s1) * (1.0 / z)
            cnt_ref[h, :, cols] = cnt
            e2_ref[h, :, cols] = jnp.exp(s2).astype(BF16)
            r2_ref[h, :, cols] = rank2.astype(BF16)


def peer_scores(xb, wq, keys, layer):
    m, d = xb.shape
    t = PEER_SCORE_TILE
    sblk = pl.BlockSpec((PEER_HEADS, PEER_NKEYS, t), lambda i: (0, 0, i))
    sshape = (PEER_HEADS, PEER_NKEYS, m)
    return pl.pallas_call(
        _peer_scores_kernel,
        grid=(m // t,),
        in_specs=[pl.BlockSpec((t, d), lambda i: (i, 0)),
                  pl.BlockSpec((None, d, d), lambda i: (layer, 0, 0)),
                  pl.BlockSpec((None, 2, PEER_NKEYS, PEER_HALF), lambda i: (layer, 0, 0, 0))],
        out_specs=[sblk, sblk, sblk, sblk],
        out_shape=[jax.ShapeDtypeStruct(sshape, F32), jax.ShapeDtypeStruct(sshape, F32),
                   jax.ShapeDtypeStruct(sshape, BF16), jax.ShapeDtypeStruct(sshape, BF16)],
        scratch_shapes=[pltpu.VMEM((PEER_TOPK, LANES), F32), pltpu.VMEM((PEER_TOPK, LANES), F32),
                        pltpu.VMEM((PEER_NCAND, LANES), F32)],
        compiler_params=_params("parallel"),
        name="peer_scores",
    )(xb, wq, keys)


PEER_TOKENS = 512
PEER_ETILE = 1024


PEER_FIRST_KEYS = PEER_ETILE // PEER_NKEYS
assert PEER_FIRST_KEYS == SUBLANES
BF16_ROWS = 2 * SUBLANES


FP8 = jnp.float8_e4m3fn
FP8_TARGET = 224.0


def _pow2_scale(amax):
    return jnp.exp2(jnp.floor(jnp.log2(FP8_TARGET / jnp.maximum(amax, 1e-30))))


def _peer_expert_kernel(sc_ref, x8_ref, e1_ref, cnt_ref, e2_in_ref, r2_in_ref, u_ref, vt_ref, o_ref,
                        acc_ref, at_ref, e2_ref, r2_ref, rows_ref, gate_ref):
    e = pl.program_id(1)
    inv_ux = sc_ref[0:1, 1:2]

    @pl.when(e == 0)
    def _():
        acc_ref[...] = jnp.zeros_like(acc_ref)
        e2_ref[...] = e2_in_ref[...]
        r2_ref[...] = r2_in_ref[...]

    first = pl.ds(pl.multiple_of(e * PEER_FIRST_KEYS, PEER_FIRST_KEYS), PEER_FIRST_KEYS)
    for h in range(PEER_HEADS):
        rows_ref[0, h] = cnt_ref[h, first, :]
        rows_ref[1, h] = e1_ref[h, first, :]
    n_jb = PEER_NKEYS // BF16_ROWS
    for ii in range(PEER_FIRST_KEYS):
        gates = [jnp.zeros((BF16_ROWS, PEER_TOKENS), BF16) for _ in range(n_jb)]
        for h in range(PEER_HEADS):
            cnt_i = jnp.broadcast_to(rows_ref[0, h, ii:ii + 1, :], (BF16_ROWS, PEER_TOKENS)).astype(BF16)
            e1_i = jnp.broadcast_to(rows_ref[1, h, ii:ii + 1, :], (BF16_ROWS, PEER_TOKENS)).astype(BF16)
            for jb in range(n_jb):
                js = pl.ds(jb * BF16_ROWS, BF16_ROWS)
                w = e2_ref[h, js, :] * e1_i
                gates[jb] = gates[jb] + jnp.where(r2_ref[h, js, :] < cnt_i, w, jnp.zeros_like(w))
        for jb in range(n_jb):
            gate_ref[pl.ds(ii * PEER_NKEYS + jb * BF16_ROWS, BF16_ROWS), :] = gates[jb]
    ht = lax.dot_general(u_ref[...], x8_ref[...], (((1,), (1,)), ((), ())), preferred_element_type=F32)
    gelu = (ht * (0.5 * inv_ux)) * (1.0 + lax.erf(ht * (inv_ux * (1.0 / math.sqrt(2.0)))))
    at_ref[...] = gelu.astype(BF16) * gate_ref[...]
    acc_ref[...] += jnp.dot(vt_ref[...], at_ref[...], preferred_element_type=F32)

    @pl.when(e == pl.num_programs(1) - 1)
    def _():
        o_ref[...] = acc_ref[...].T


def peer_experts(scales, x8, e1, cnt, e2, r2, u_tab, vt_tab, layer):
    m, d = x8.shape
    t = PEER_TOKENS
    sblk = pl.BlockSpec((PEER_HEADS, PEER_NKEYS, t), lambda i, e: (0, 0, i))
    return pl.pallas_call(
        _peer_expert_kernel,
        grid=(m // t, PEER_EXPERTS // PEER_ETILE),
        in_specs=[pl.BlockSpec((None, 1, LANES), lambda i, e: (layer, 0, 0)),
                  pl.BlockSpec((t, d), lambda i, e: (i, 0)),
                  sblk, sblk, sblk, sblk,
                  pl.BlockSpec((None, PEER_ETILE, d), lambda i, e: (layer, e, 0)),
                  pl.BlockSpec((None, d, PEER_ETILE), lambda i, e: (layer, 0, e))],
        out_specs=pl.BlockSpec((t, d), lambda i, e: (i, 0)),
        out_shape=jax.ShapeDtypeStruct((m, d), F32),
        scratch_shapes=[pltpu.VMEM((d, t), F32),
                        pltpu.VMEM((PEER_ETILE, t), BF16),
                        pltpu.VMEM((PEER_HEADS, PEER_NKEYS, t), BF16),
                        pltpu.VMEM((PEER_HEADS, PEER_NKEYS, t), BF16),
                        pltpu.VMEM((2, PEER_HEADS, PEER_FIRST_KEYS, t), F32),
                        pltpu.VMEM((PEER_ETILE, t), BF16)],
        compiler_params=_params("parallel", "arbitrary"),
        name="peer_experts",
    )(scales, x8, e1, cnt, e2, r2, u_tab, vt_tab)


MM_TILE_M = 1024
MM_TILE_N = 512
MM_TILE_N_WIDE = 1024


def _trunk(x, bsz, seq, p):
    m = bsz * seq
    x, xb = layer_norm_rows(x, p['ln_in_g'], p['ln_in_b'])
    cos, sin = rotary_tables(seq)
    for l in range(DEPTH):
        w_in = p['w_in']
        ua = matmul(xb, w_in, l, 0, S5_WIDTH, F32, MM_TILE_M, S5_WIDTH, time_major_batch=(bsz, seq), name="mm_in_s5")
        ua = ua.reshape(m, S5_WIDTH)
        ub = matmul(xb, w_in, l, S5_WIDTH, POOL_WIDTH, F32, MM_TILE_M, MM_TILE_N, name="mm_in_pool")
        hq = matmul(xb, w_in, l, S5_WIDTH + POOL_WIDTH, 4 * RET_WIDTH, BF16, MM_TILE_M, MM_TILE_N_WIDE,
                    name="mm_in_ret")
        y_fwd = s5_direction(ua, *p['s5_fwd'][l], bsz, False)
        ya = s5_direction(ua, *p['s5_bwd'][l], bsz, True,
                          fin=(y_fwd, p['s5_d'][l], p['s5_w_glu'][l], p['s5_b_glu'][l]))
        yb = pool_mixer(ub, p['pool_w'][l], p['pool_scale'][l], seq)
        part = retention_direction(hq, cos, sin, bsz, seq, False)
        yc = retention_direction(hq, cos, sin, bsz, seq, True, part=part)
        x, xb, x8 = mix_ln(p['peer_scales'], x, ya.reshape(seq, bsz * S5_WIDTH), yb, yc, p['w_out'], l,
                           p['ln1_g'][l], p['ln1_b'][l], bsz, seq)
        e1, cnt, e2, r2 = peer_scores(xb, p['peer_w_q'], p['peer_keys'], l)
        ff = peer_experts(p['peer_scales'], x8, e1, cnt, e2, r2, p['peer_u'], p['peer_vt'], l)
        x, xb = resid_ln(x, ff, p['ln2_g'][l], p['ln2_b'][l])
    return x


def kernel(x_prompt, x_sample, ln_in_g, ln_in_b, w_in, s5_lambda_re, s5_lambda_im, s5_log_step, s5_b_re, s5_b_im, s5_c_re, s5_c_im, s5_d, s5_w_glu, s5_b_glu, pool_w, pool_scale, w_out, ln1_g, ln1_b, peer_w_q, peer_sub_keys, peer_u, peer_v, ln2_g, ln2_b):
    d = D_MODEL

    def s5_dir(direction):
        return [s5_weights(s5_lambda_re[l, direction], s5_lambda_im[l, direction], s5_log_step[l, direction],
                           s5_b_re[l, direction], s5_b_im[l, direction], s5_c_re[l, direction],
                           s5_c_im[l, direction]) for l in range(DEPTH)]

    u_scale = _pow2_scale(jnp.max(jnp.abs(peer_u), axis=(1, 2), keepdims=True))
    x_scale = _pow2_scale(math.sqrt(d) * jnp.max(jnp.abs(ln1_g), axis=1) + jnp.max(jnp.abs(ln1_b), axis=1))
    peer_scales = jnp.zeros((DEPTH, 1, LANES), F32)
    peer_scales = peer_scales.at[:, 0, 0].set(x_scale)
    peer_scales = peer_scales.at[:, 0, 1].set(1.0 / (u_scale[:, 0, 0] * x_scale))

    p = dict(
        ln_in_g=ln_in_g, ln_in_b=ln_in_b,
        w_in=w_in.astype(BF16),
        s5_fwd=s5_dir(0), s5_bwd=s5_dir(1),
        s5_d=s5_d.reshape(DEPTH, 1, S5_WIDTH), s5_w_glu=s5_w_glu.astype(BF16),
        s5_b_glu=s5_b_glu.reshape(DEPTH, 1, S5_WIDTH),
        pool_w=pool_w.astype(BF16), pool_scale=pool_scale.reshape(DEPTH, 1, POOL_WIDTH),
        w_out=w_out.astype(BF16),
        ln1_g=ln1_g.reshape(DEPTH, 1, d), ln1_b=ln1_b.reshape(DEPTH, 1, d),
        peer_w_q=peer_w_q.astype(BF16), peer_keys=peer_sub_keys.astype(BF16),
        peer_u=(peer_u * u_scale).astype(FP8), peer_vt=jnp.swapaxes(peer_v, 1, 2).astype(BF16),
        peer_scales=peer_scales,
        ln2_g=ln2_g.reshape(DEPTH, 1, d), ln2_b=ln2_b.reshape(DEPTH, 1, d),
    )
    outs = []
    for x in (x_prompt, x_sample):
        bsz, seq, _ = x.shape
        outs.append(_trunk(x.reshape(bsz * seq, d), bsz, seq, p).reshape(bsz, seq, d))
    return tuple(outs)
```

```python
import functools
import math

import jax
import jax.numpy as jnp
from jax import lax
from jax.experimental import pallas as pl
from jax.experimental.pallas import tpu as pltpu

F32 = jnp.float32
BF16 = jnp.bfloat16

D_MODEL = 2048
DEPTH = 4
S5_WIDTH = 512
POOL_WIDTH = 512
RET_WIDTH = 1024
S5_GROUP = 16
S5_GROUPS = 32
S5_STATE = 64
S5_NSTATE = S5_GROUPS * S5_STATE
POOL_WINDOWS = (2, 4, 8, 16)
POOL_GROUP = 128
RET_HEADS = 4
RET_HEAD_DIM = 256
RET_CHUNK = 128
PEER_HEADS = 8
PEER_HALF = 128
PEER_NKEYS = 128
PEER_EXPERTS = PEER_NKEYS * PEER_NKEYS
PEER_TOPK = 16
ALPHA = (2 * DEPTH) ** 0.25
LN_EPS = 1e-5
GN_EPS = 1e-6

LANES = 128
SUBLANES = 8
VMEM_LIMIT = 56 * 1024 * 1024


def _params(*sem, flags=None):
    return pltpu.CompilerParams(dimension_semantics=sem, vmem_limit_bytes=VMEM_LIMIT, flags=flags)


def _layer_norm(y, g, b):
    mu = jnp.mean(y, axis=-1, keepdims=True)
    yc = y - mu
    var = jnp.mean(yc * yc, axis=-1, keepdims=True)
    return yc * lax.rsqrt(var + LN_EPS) * g + b


def _gelu(x):
    return 0.5 * x * (1.0 + lax.erf(x * (1.0 / math.sqrt(2.0))))


def _ln_kernel(x_ref, g_ref, b_ref, y_ref, yb_ref):
    y = _layer_norm(x_ref[...], g_ref[...], b_ref[...])
    y_ref[...] = y
    yb_ref[...] = y.astype(BF16)


def layer_norm_rows(x, g, b, tm=256):
    m, d = x.shape
    return pl.pallas_call(
        _ln_kernel,
        grid=(m // tm,),
        in_specs=[pl.BlockSpec((tm, d), lambda i: (i, 0)),
                  pl.BlockSpec((1, d), lambda i: (0, 0)),
                  pl.BlockSpec((1, d), lambda i: (0, 0))],
        out_specs=[pl.BlockSpec((tm, d), lambda i: (i, 0)),
                   pl.BlockSpec((tm, d), lambda i: (i, 0))],
        out_shape=[jax.ShapeDtypeStruct((m, d), F32), jax.ShapeDtypeStruct((m, d), BF16)],
        compiler_params=_params("parallel"),
        name="ln_in",
    )(x, g.reshape(1, d), b.reshape(1, d))


def _mm_kernel(x_ref, w_ref, o_ref):
    o_ref[...] = jnp.dot(x_ref[...], w_ref[...], preferred_element_type=F32).astype(o_ref.dtype)


def matmul(xb, w, layer, col0, n, out_dtype, tm, tn, time_major_batch=None, name="mm"):
    m, k = xb.shape
    assert col0 % tn == 0 and n % tn == 0 and w.shape[1] == k
    j0 = col0 // tn
    if time_major_batch is None:
        out_shape = jax.ShapeDtypeStruct((m, n), out_dtype)
        out_spec = pl.BlockSpec((tm, tn), lambda i, j: (i, j))
    else:
        bsz, seq = time_major_batch
        assert n == tn and seq % tm == 0 and bsz * seq == m
        tiles = seq // tm
        out_shape = jax.ShapeDtypeStruct((seq, bsz * n), out_dtype)
        out_spec = pl.BlockSpec((tm, tn), lambda i, j: (i % tiles, i // tiles))
    return pl.pallas_call(
        _mm_kernel,
        grid=(m // tm, n // tn),
        in_specs=[pl.BlockSpec((tm, k), lambda i, j: (i, 0)),
                  pl.BlockSpec((None, k, tn), lambda i, j: (layer, 0, j0 + j))],
        out_specs=out_spec,
        out_shape=out_shape,
        compiler_params=_params("parallel", "arbitrary"),
        name=name,
    )(xb, w)


S5_ROWS = 512
S5_SLABS = 4
S5_SLAB_IN = S5_WIDTH // S5_SLABS
S5_COLS = S5_NSTATE // S5_SLABS


def _s5_kernel(*refs, bsz, reverse, finalize):
    if finalize:
        (u_ref, ar_ref, ai_ref, bslab_ref, cslab_ref, yprev_ref, d_ref, wglu_ref, bglu_ref,
         o_ref, xr_ref, xi_ref, bu_ref) = refs
    else:
        u_ref, ar_ref, ai_ref, bslab_ref, cslab_ref, o_ref, xr_ref, xi_ref, bu_ref = refs

    @pl.when(pl.program_id(0) == 0)
    def _():
        xr_ref[...] = jnp.zeros_like(xr_ref)
        xi_ref[...] = jnp.zeros_like(xi_ref)

    u = u_ref[...]
    ub = u.astype(BF16)
    steps = S5_ROWS // bsz
    ys = []
    for sl in range(S5_SLABS):
        cr = pl.ds(2 * sl * S5_COLS, S5_COLS)
        ci = pl.ds((2 * sl + 1) * S5_COLS, S5_COLS)
        both = pl.ds(2 * sl * S5_COLS, 2 * S5_COLS)
        st = pl.ds(sl * S5_COLS, S5_COLS)
        bu_ref[:, both] = jnp.dot(ub[:, sl * S5_SLAB_IN:(sl + 1) * S5_SLAB_IN], bslab_ref[sl],
                                  preferred_element_type=F32)
        ar = jnp.broadcast_to(ar_ref[:, st], (bsz, S5_COLS))
        ai = jnp.broadcast_to(ai_ref[:, st], (bsz, S5_COLS))

        def step(s, carry, cr=cr, ci=ci, ar=ar, ai=ai):
            xr, xi = carry
            t = (steps - 1 - s) if reverse else s
            rows = pl.ds(pl.multiple_of(t * bsz, bsz), bsz)
            nr = ar * xr - ai * xi + bu_ref[rows, cr]
            ni = ar * xi + ai * xr + bu_ref[rows, ci]
            bu_ref[rows, cr] = nr
            bu_ref[rows, ci] = ni
            return nr, ni

        xr, xi = lax.fori_loop(0, steps, step, (xr_ref[:, st], xi_ref[:, st]), unroll=8)
        xr_ref[:, st] = xr
        xi_ref[:, st] = xi
        ys.append(jnp.dot(bu_ref[:, both].astype(BF16), cslab_ref[sl], preferred_element_type=F32))
    y = jnp.concatenate(ys, axis=1)
    if finalize:
        y = _gelu(y + yprev_ref[...] + d_ref[...] * u)
        z = jnp.dot(y.astype(BF16), wglu_ref[...], preferred_element_type=F32) + bglu_ref[...]
        o_ref[...] = (y * jax.nn.sigmoid(z)).astype(o_ref.dtype)
    else:
        o_ref[...] = y


def s5_direction(u, abar_r, abar_i, bslab, cslab, bsz, reverse, fin=None):
    m = u.shape[0]
    nc = m // S5_ROWS
    cidx = (lambda c: (nc - 1 - c, 0)) if reverse else (lambda c: (c, 0))
    const = lambda c: (0, 0)
    in_specs = [pl.BlockSpec((S5_ROWS, S5_WIDTH), cidx),
                pl.BlockSpec((1, S5_NSTATE), const),
                pl.BlockSpec((1, S5_NSTATE), const),
                pl.BlockSpec((S5_SLABS, S5_SLAB_IN, 2 * S5_COLS), lambda c: (0, 0, 0)),
                pl.BlockSpec((S5_SLABS, 2 * S5_COLS, S5_SLAB_IN), lambda c: (0, 0, 0))]
    args = [u, abar_r, abar_i, bslab, cslab]
    if fin is not None:
        y_prev, d, w_glu, b_glu = fin
        in_specs += [pl.BlockSpec((S5_ROWS, S5_WIDTH), cidx),
                     pl.BlockSpec((1, S5_WIDTH), const),
                     pl.BlockSpec((S5_WIDTH, S5_WIDTH), const),
                     pl.BlockSpec((1, S5_WIDTH), const)]
        args += [y_prev, d, w_glu, b_glu]
    return pl.pallas_call(
        functools.partial(_s5_kernel, bsz=bsz, reverse=reverse, finalize=fin is not None),
        grid=(nc,),
        in_specs=in_specs,
        out_specs=pl.BlockSpec((S5_ROWS, S5_WIDTH), cidx),
        out_shape=jax.ShapeDtypeStruct((m, S5_WIDTH), BF16 if fin is not None else F32),
        scratch_shapes=[pltpu.VMEM((bsz, S5_NSTATE), F32),
                        pltpu.VMEM((bsz, S5_NSTATE), F32),
                        pltpu.VMEM((S5_ROWS, 2 * S5_NSTATE), F32)],
        compiler_params=_params("arbitrary"),
        name="s5_bwd" if reverse else "s5_fwd",
    )(*args)


def s5_weights(lam_re, lam_im, log_step, b_re, b_im, c_re, c_im):
    dt = jnp.exp(log_step)[:, None]
    mag = jnp.exp(lam_re * dt)
    abar_r = mag * jnp.cos(lam_im * dt)
    abar_i = mag * jnp.sin(lam_im * dt)
    den = lam_re * lam_re + lam_im * lam_im
    zr = ((abar_r - 1.0) * lam_re + abar_i * lam_im) / den
    zi = (abar_i * lam_re - (abar_r - 1.0) * lam_im) / den
    bb_r = zr[..., None] * b_re - zi[..., None] * b_im
    bb_i = zr[..., None] * b_im + zi[..., None] * b_re
    eye = jnp.eye(S5_GROUPS, dtype=F32)

    def b_block(bb):
        return jnp.einsum('gpc,gh->gchp', bb, eye).reshape(S5_WIDTH, S5_NSTATE)

    def c_block(cc):
        return jnp.einsum('gcp,gh->gphc', cc, eye).reshape(S5_NSTATE, S5_WIDTH)

    def slab(mat, sl, rows, cols):
        return mat[sl * rows:(sl + 1) * rows, sl * cols:(sl + 1) * cols]

    br, bi, cr, ci = b_block(bb_r), b_block(bb_i), c_block(c_re), c_block(c_im)
    bslab = jnp.stack([jnp.concatenate([slab(br, sl, S5_SLAB_IN, S5_COLS), slab(bi, sl, S5_SLAB_IN, S5_COLS)], axis=1)
                       for sl in range(S5_SLABS)]).astype(BF16)
    cslab = jnp.stack([jnp.concatenate([slab(cr, sl, S5_COLS, S5_SLAB_IN), -slab(ci, sl, S5_COLS, S5_SLAB_IN)], axis=0)
                       for sl in range(S5_SLABS)]).astype(BF16)
    return abar_r.reshape(1, S5_NSTATE), abar_i.reshape(1, S5_NSTATE), bslab, cslab


POOL_TILE = 256
POOL_HALO = 64


def _pool_kernel(prev_ref, u_ref, next_ref, w_ref, scale_ref, o_ref, *, tiles_per_seq, seq_len):
    t0 = (pl.program_id(0) % tiles_per_seq) * POOL_TILE
    u = u_ref[...]
    ext = jnp.concatenate([prev_ref[...], u, next_ref[...]], axis=0)
    n_ext = POOL_TILE + 2 * POOL_HALO
    t_out = t0 + lax.broadcasted_iota(jnp.int32, (POOL_TILE, 1), 0)
    t_src = t0 - POOL_HALO + lax.broadcasted_iota(jnp.int32, (1, n_ext), 1)
    outs = []
    for gi, w in enumerate(POOL_WINDOWS):
        lo = jnp.maximum(t_out - w // 2, 0)
        hi = jnp.minimum(t_out + w // 2 - 1, seq_len - 1)
        band = jnp.where((t_src >= lo) & (t_src <= hi), 1.0, 0.0).astype(BF16)
        cols = slice(gi * POOL_GROUP, (gi + 1) * POOL_GROUP)
        head = ext[:, cols].astype(BF16)
        rest = (ext[:, cols] - head.astype(F32)).astype(BF16)
        tot = (jnp.dot(band, head, preferred_element_type=F32) + jnp.dot(band, rest, preferred_element_type=F32))
        p = tot / (hi - lo + 1).astype(F32) - u[:, cols]
        outs.append(jnp.dot(p.astype(BF16), w_ref[gi], preferred_element_type=F32))
    o_ref[...] = (jnp.concatenate(outs, axis=1) * scale_ref[...]).astype(o_ref.dtype)


def pool_mixer(u, w, scale, seq_len):
    m = u.shape[0]
    tiles_per_seq = seq_len // POOL_TILE
    r = POOL_TILE // POOL_HALO
    last = m // POOL_HALO - 1
    return pl.pallas_call(
        functools.partial(_pool_kernel, tiles_per_seq=tiles_per_seq, seq_len=seq_len),
        grid=(m // POOL_TILE,),
        in_specs=[pl.BlockSpec((POOL_HALO, POOL_WIDTH), lambda i: (jnp.maximum(i * r - 1, 0), 0)),
                  pl.BlockSpec((POOL_TILE, POOL_WIDTH), lambda i: (i, 0)),
                  pl.BlockSpec((POOL_HALO, POOL_WIDTH), lambda i: (jnp.minimum((i + 1) * r, last), 0)),
                  pl.BlockSpec((len(POOL_WINDOWS), POOL_GROUP, POOL_GROUP), lambda i: (0, 0, 0)),
                  pl.BlockSpec((1, POOL_WIDTH), lambda i: (0, 0))],
        out_specs=pl.BlockSpec((POOL_TILE, POOL_WIDTH), lambda i: (i, 0)),
        out_shape=jax.ShapeDtypeStruct((m, POOL_WIDTH), BF16),
        compiler_params=_params("parallel"),
        name="pool",
    )(u, u, u, w, scale)


def _ret_kernel(*refs, reverse, finalize):
    if finalize:
        q_ref, k_ref, v_ref, cos_ref, sin_ref, g_ref, part_ref, o_ref, state_ref = refs
    else:
        q_ref, k_ref, v_ref, cos_ref, sin_ref, o_ref, state_ref = refs

    @pl.when(pl.program_id(1) == 0)
    def _():
        state_ref[...] = jnp.zeros_like(state_ref)

    cos = cos_ref[...]
    sin = sin_ref[...]
    d = RET_HEAD_DIM
    half = d // 2
    n = RET_CHUNK

    def rotary(x):
        x1, x2 = x[:, :half], x[:, half:]
        return jnp.concatenate([x1 * cos - x2 * sin, x2 * cos + x1 * sin], axis=1)

    pos = lax.broadcasted_iota(jnp.int32, (n, 1), 0).astype(F32)
    dist = jnp.abs(lax.broadcasted_iota(jnp.int32, (n, n), 0)
                   - lax.broadcasted_iota(jnp.int32, (n, n), 1)).astype(F32)
    for h in range(RET_HEADS):
        cs = slice(h * d, (h + 1) * d)
        log_gamma = math.log(1.0 - 2.0 ** (-5.0 - h))
        q = rotary(q_ref[:, cs].astype(F32))
        k = rotary(k_ref[:, cs].astype(F32)) * (d ** -0.5)
        vb = v_ref[:, cs]
        if reverse:
            q_decay = jnp.exp(log_gamma * (n - pos))
            k_decay = jnp.exp(log_gamma * pos)
        else:
            q_decay = jnp.exp(log_gamma * (pos + 1.0))
            k_decay = jnp.exp(log_gamma * (n - 1.0 - pos))
        state = state_ref[h]
        o = jnp.dot((q * q_decay).astype(BF16), state.astype(BF16), preferred_element_type=F32)
        if not reverse:
            scores = lax.dot_general(q.astype(BF16), k.astype(BF16), (((1,), (1,)), ((), ())),
                                     preferred_element_type=F32) * jnp.exp(log_gamma * dist)
            o = o + jnp.dot(scores.astype(BF16), vb, preferred_element_type=F32)
        kv = lax.dot_general((k * k_decay).astype(BF16), vb, (((0,), (0,)), ((), ())),
                             preferred_element_type=F32)
        state_ref[h] = math.exp(log_gamma * n) * state + kv
        if finalize:
            o = o + part_ref[:, cs]
            mu = jnp.mean(o, axis=-1, keepdims=True)
            oc = o - mu
            var = jnp.mean(oc * oc, axis=-1, keepdims=True)
            g = g_ref[:, cs].astype(F32)
            o_ref[:, cs] = (g * jax.nn.sigmoid(g) * (oc * lax.rsqrt(var + GN_EPS))).astype(o_ref.dtype)
        else:
            o_ref[:, cs] = o


def retention_direction(hq, cos, sin, bsz, seq, reverse, part=None):
    m = hq.shape[0]
    nc = seq // RET_CHUNK
    w = RET_WIDTH

    def rows(b, c):
        return b * nc + ((nc - 1 - c) if reverse else c)

    def col(j):
        return pl.BlockSpec((RET_CHUNK, w), lambda b, c, j=j: (rows(b, c), j))

    tab = pl.BlockSpec((RET_CHUNK, RET_HEAD_DIM // 2), lambda b, c: ((nc - 1 - c) if reverse else c, 0))
    in_specs = [col(0), col(1), col(2), tab, tab]
    args = [hq, hq, hq, cos, sin]
    if part is not None:
        in_specs += [col(3), col(0)]
        args += [hq, part]
    return pl.pallas_call(
        functools.partial(_ret_kernel, reverse=reverse, finalize=part is not None),
        grid=(bsz, nc),
        in_specs=in_specs,
        out_specs=col(0),
        out_shape=jax.ShapeDtypeStruct((m, w), BF16 if part is not None else F32),
        scratch_shapes=[pltpu.VMEM((RET_HEADS, RET_HEAD_DIM, RET_HEAD_DIM), F32)],
        compiler_params=_params("parallel", "arbitrary"),
        name="ret_bwd" if reverse else "ret_fwd",
    )(*args)


def rotary_tables(seq):
    half = RET_HEAD_DIM // 2
    inv = 1.0 / (10000.0 ** jnp.linspace(0.0, 1.0, half, dtype=F32))
    ang = jnp.arange(seq, dtype=F32)[:, None] * inv[None, :]
    return jnp.cos(ang), jnp.sin(ang)


MIX_TILE = 512


def _mix_kernel(sc_ref, x_ref, ya_ref, yb_ref, yc_ref, wa_ref, wb_ref, wc_ref, g_ref, b_ref, o_ref, ob_ref, o8_ref):
    mix = (jnp.dot(ya_ref[...], wa_ref[...], preferred_element_type=F32)
           + jnp.dot(yb_ref[...], wb_ref[...], preferred_element_type=F32)
           + jnp.dot(yc_ref[...], wc_ref[...], preferred_element_type=F32))
    y = _layer_norm(ALPHA * x_ref[...] + mix, g_ref[...], b_ref[...])
    o_ref[...] = y
    ob_ref[...] = y.astype(BF16)
    o8_ref[...] = (y * sc_ref[0:1, 0:1]).astype(FP8)


def mix_ln(scales, x, ya_tb, yb, yc, w_out, layer, g, b, bsz, seq):
    m, d = x.shape
    tm = MIX_TILE
    tiles = seq // tm
    row = lambda i: (i, 0)
    const = lambda i: (0, 0)
    return pl.pallas_call(
        _mix_kernel,
        grid=(m // tm,),
        in_specs=[pl.BlockSpec((None, 1, LANES), lambda i: (layer, 0, 0)),
                  pl.BlockSpec((tm, d), row),
                  pl.BlockSpec((tm, S5_WIDTH), lambda i: (i % tiles, i // tiles)),
                  pl.BlockSpec((tm, POOL_WIDTH), row),
                  pl.BlockSpec((tm, RET_WIDTH), row),
                  pl.BlockSpec((None, S5_WIDTH, d), lambda i: (layer, 0, 0)),
                  pl.BlockSpec((None, POOL_WIDTH, d), lambda i: (layer, 1, 0)),
                  pl.BlockSpec((None, RET_WIDTH, d), lambda i: (layer, 1, 0)),
                  pl.BlockSpec((1, d), const),
                  pl.BlockSpec((1, d), const)],
        out_specs=[pl.BlockSpec((tm, d), row), pl.BlockSpec((tm, d), row), pl.BlockSpec((tm, d), row)],
        out_shape=[jax.ShapeDtypeStruct((m, d), F32), jax.ShapeDtypeStruct((m, d), BF16),
                   jax.ShapeDtypeStruct((m, d), FP8)],
        compiler_params=_params("parallel"),
        name="mix_ln",
    )(scales, x, ya_tb, yb, yc, w_out, w_out, w_out, g, b)


def _resid_ln_kernel(x_ref, f_ref, g_ref, b_ref, o_ref, ob_ref):
    y = _layer_norm(ALPHA * x_ref[...] + f_ref[...], g_ref[...], b_ref[...])
    o_ref[...] = y
    ob_ref[...] = y.astype(BF16)


def resid_ln(x, f, g, b, tm=256):
    m, d = x.shape
    row = lambda i: (i, 0)
    const = lambda i: (0, 0)
    return pl.pallas_call(
        _resid_ln_kernel,
        grid=(m // tm,),
        in_specs=[pl.BlockSpec((tm, d), row), pl.BlockSpec((tm, d), row),
                  pl.BlockSpec((1, d), const), pl.BlockSpec((1, d), const)],
        out_specs=[pl.BlockSpec((tm, d), row), pl.BlockSpec((tm, d), row)],
        out_shape=[jax.ShapeDtypeStruct((m, d), F32), jax.ShapeDtypeStruct((m, d), BF16)],
        compiler_params=_params("parallel"),
        name="resid_ln",
    )(x, f, g, b)


PEER_SCORE_TILE = 512
NEG_BIG = -3.0e38
PEER_NCAND = PEER_TOPK + SUBLANES * (PEER_TOPK - 1)


def _top_desc(s, n, out_ref, with_rank):
    rank = jnp.full(s.shape, float(n), F32) if with_rank else None
    for r in range(n):
        m = jnp.max(s, axis=0, keepdims=True)
        out_ref[r:r + 1, :] = m
        hit = s >= m
        if with_rank:
            rank = jnp.where(hit, float(r), rank)
        if r < n - 1:
            s = jnp.where(hit, NEG_BIG, s)
    return rank


def _peer_scores_kernel(xb_ref, wq_ref, keys_ref, e1_ref, cnt_ref, e2_ref, r2_ref, a_scr, b_scr, cand_scr):
    q = jnp.dot(xb_ref[...], wq_ref[...], preferred_element_type=F32)
    nt = (((1,), (1,)), ((), ()))
    for h in range(PEER_HEADS):
        c0 = 2 * h * PEER_HALF
        s1_all = lax.dot_general(keys_ref[0], q[:, c0:c0 + PEER_HALF].astype(BF16), nt,
                                 preferred_element_type=F32)
        s2_all = lax.dot_general(keys_ref[1], q[:, c0 + PEER_HALF:c0 + 2 * PEER_HALF].astype(BF16), nt,
                                 preferred_element_type=F32)
        for lb in range(PEER_SCORE_TILE // LANES):
            cols = pl.ds(lb * LANES, LANES)
            s1 = s1_all[:, lb * LANES:(lb + 1) * LANES]
            s2 = s2_all[:, lb * LANES:(lb + 1) * LANES]
            s1 = s1 - jnp.max(s1, axis=0, keepdims=True)
            s2 = s2 - jnp.max(s2, axis=0, keepdims=True)
            _top_desc(s1, PEER_TOPK, a_scr, False)
            rank2 = _top_desc(s2, PEER_TOPK, b_scr, True)
            cand_scr[0:PEER_TOPK, :] = a_scr[0:1, :] + b_scr[...]
            for p in range(1, PEER_TOPK):
                r0 = PEER_TOPK + SUBLANES * (p - 1)
                cand_scr[r0:r0 + SUBLANES, :] = a_scr[p:p + 1, :] + b_scr[0:SUBLANES, :]
            cand = cand_scr[...]
            c = cand
            for _ in range(PEER_TOPK - 1):
                c = jnp.where(c >= jnp.max(c, axis=0, keepdims=True), NEG_BIG, c)
            tau = jnp.max(c, axis=0, keepdims=True)
            z = jnp.sum(jnp.where(cand >= tau, jnp.exp(cand), 0.0), axis=0, keepdims=True)
            b_top = b_scr[...]
            cnt = jnp.zeros(s1.shape, F32)
            for p in range(PEER_TOPK):
                a_p = a_scr[p:p + 1, :]
                c_p = jnp.sum(jnp.where(a_p + b_top >= tau, 1.0, 0.0), axis=0, keepdims=True)
                cnt = jnp.where(s1 == a_p, c_p, cnt)
            e1_ref[h, :, cols] = jnp.exp(s1) * (1.0 / z)
            cnt_ref[h, :, cols] = cnt
            e2_ref[h, :, cols] = jnp.exp(s2).astype(BF16)
            r2_ref[h, :, cols] = rank2.astype(BF16)


def peer_scores(xb, wq, keys, layer):
    m, d = xb.shape
    t = PEER_SCORE_TILE
    sblk = pl.BlockSpec((PEER_HEADS, PEER_NKEYS, t), lambda i: (0, 0, i))
    sshape = (PEER_HEADS, PEER_NKEYS, m)
    return pl.pallas_call(
        _peer_scores_kernel,
        grid=(m // t,),
        in_specs=[pl.BlockSpec((t, d), lambda i: (i, 0)),
                  pl.BlockSpec((None, d, d), lambda i: (layer, 0, 0)),
                  pl.BlockSpec((None, 2, PEER_NKEYS, PEER_HALF), lambda i: (layer, 0, 0, 0))],
        out_specs=[sblk, sblk, sblk, sblk],
        out_shape=[jax.ShapeDtypeStruct(sshape, F32), jax.ShapeDtypeStruct(sshape, F32),
                   jax.ShapeDtypeStruct(sshape, BF16), jax.ShapeDtypeStruct(sshape, BF16)],
        scratch_shapes=[pltpu.VMEM((PEER_TOPK, LANES), F32), pltpu.VMEM((PEER_TOPK, LANES), F32),
                        pltpu.VMEM((PEER_NCAND, LANES), F32)],
        compiler_params=_params("parallel"),
        name="peer_scores",
    )(xb, wq, keys)


PEER_TOKENS = 512
PEER_ETILE = 1024


PEER_FIRST_KEYS = PEER_ETILE // PEER_NKEYS
assert PEER_FIRST_KEYS == SUBLANES
BF16_ROWS = 2 * SUBLANES


FP8 = jnp.float8_e4m3fn
FP8_TARGET = 224.0


def _pow2_scale(amax):
    return jnp.exp2(jnp.floor(jnp.log2(FP8_TARGET / jnp.maximum(amax, 1e-30))))


def _peer_expert_kernel(sc_ref, x8_ref, e1_ref, cnt_ref, e2_in_ref, r2_in_ref, u_ref, vt_ref, x_ref, g_ref, b_ref,
                        o_ref, ob_ref, acc_ref, at_ref, e2_ref, r2_ref, rows_ref, gate_ref):
    e = pl.program_id(1)
    inv_ux = sc_ref[0:1, 1:2]

    @pl.when(e == 0)
    def _():
        acc_ref[...] = jnp.zeros_like(acc_ref)
        e2_ref[...] = e2_in_ref[...]
        r2_ref[...] = r2_in_ref[...]

    first = pl.ds(pl.multiple_of(e * PEER_FIRST_KEYS, PEER_FIRST_KEYS), PEER_FIRST_KEYS)
    for h in range(PEER_HEADS):
        rows_ref[0, h] = cnt_ref[h, first, :]
        rows_ref[1, h] = e1_ref[h, first, :]
    n_jb = PEER_NKEYS // BF16_ROWS
    for ii in range(PEER_FIRST_KEYS):
        gates = [jnp.zeros((BF16_ROWS, PEER_TOKENS), BF16) for _ in range(n_jb)]
        for h in range(PEER_HEADS):
            cnt_i = jnp.broadcast_to(rows_ref[0, h, ii:ii + 1, :], (BF16_ROWS, PEER_TOKENS)).astype(BF16)
            e1_i = jnp.broadcast_to(rows_ref[1, h, ii:ii + 1, :], (BF16_ROWS, PEER_TOKENS)).astype(BF16)
            for jb in range(n_jb):
                js = pl.ds(jb * BF16_ROWS, BF16_ROWS)
                w = e2_ref[h, js, :] * e1_i
                gates[jb] = gates[jb] + jnp.where(r2_ref[h, js, :] < cnt_i, w, jnp.zeros_like(w))
        for jb in range(n_jb):
            gate_ref[pl.ds(ii * PEER_NKEYS + jb * BF16_ROWS, BF16_ROWS), :] = gates[jb]
    ht = lax.dot_general(u_ref[...], x8_ref[...], (((1,), (1,)), ((), ())), preferred_element_type=F32)
    gelu = (ht * (0.5 * inv_ux)) * (1.0 + lax.erf(ht * (inv_ux * (1.0 / math.sqrt(2.0)))))
    at_ref[...] = gelu.astype(BF16) * gate_ref[...]
    acc_ref[...] += jnp.dot(vt_ref[...], at_ref[...], preferred_element_type=F32)

    @pl.when(e == pl.num_programs(1) - 1)
    def _():
        y = _layer_norm(ALPHA * x_ref[...] + acc_ref[...].T, g_ref[...], b_ref[...])
        o_ref[...] = y
        ob_ref[...] = y.astype(BF16)


def peer_experts(scales, x8, e1, cnt, e2, r2, u_tab, vt_tab, layer, x, g, b):
    m, d = x8.shape
    t = PEER_TOKENS
    once = pl.Buffered(1)
    sblk = pl.BlockSpec((PEER_HEADS, PEER_NKEYS, t), lambda i, e: (0, 0, i), pipeline_mode=once)
    tok = pl.BlockSpec((t, d), lambda i, e: (i, 0))
    const = pl.BlockSpec((1, d), lambda i, e: (0, 0))
    return pl.pallas_call(
        _peer_expert_kernel,
        grid=(m // t, PEER_EXPERTS // PEER_ETILE),
        in_specs=[pl.BlockSpec((None, 1, LANES), lambda i, e: (layer, 0, 0)),
                  pl.BlockSpec((t, d), lambda i, e: (i, 0), pipeline_mode=once),
                  sblk, sblk, sblk, sblk,
                  pl.BlockSpec((None, PEER_ETILE, d), lambda i, e: (layer, e, 0)),
                  pl.BlockSpec((None, d, PEER_ETILE), lambda i, e: (layer, 0, e)),
                  pl.BlockSpec((t, d), lambda i, e: (i, 0), pipeline_mode=once), const, const],
        out_specs=[tok, tok],
        out_shape=[jax.ShapeDtypeStruct((m, d), F32), jax.ShapeDtypeStruct((m, d), BF16)],
        scratch_shapes=[pltpu.VMEM((d, t), F32),
                        pltpu.VMEM((PEER_ETILE, t), BF16),
                        pltpu.VMEM((PEER_HEADS, PEER_NKEYS, t), BF16),
                        pltpu.VMEM((PEER_HEADS, PEER_NKEYS, t), BF16),
                        pltpu.VMEM((2, PEER_HEADS, PEER_FIRST_KEYS, t), F32),
                        pltpu.VMEM((PEER_ETILE, t), BF16)],
        compiler_params=_params("parallel", "arbitrary"),
        name="peer_experts",
    )(scales, x8, e1, cnt, e2, r2, u_tab, vt_tab, x, g, b)


MM_TILE_M = 1024
MM_TILE_N = 512
MM_TILE_N_WIDE = 1024


def _trunk(x, bsz, seq, p):
    m = bsz * seq
    x, xb = layer_norm_rows(x, p['ln_in_g'], p['ln_in_b'])
    cos, sin = rotary_tables(seq)
    for l in range(DEPTH):
        w_in = p['w_in']
        ua = matmul(xb, w_in, l, 0, S5_WIDTH, F32, MM_TILE_M, S5_WIDTH, time_major_batch=(bsz, seq), name="mm_in_s5")
        ua = ua.reshape(m, S5_WIDTH)
        ub = matmul(xb, w_in, l, S5_WIDTH, POOL_WIDTH, F32, MM_TILE_M, MM_TILE_N, name="mm_in_pool")
        hq = matmul(xb, w_in, l, S5_WIDTH + POOL_WIDTH, 4 * RET_WIDTH, BF16, MM_TILE_M, MM_TILE_N_WIDE,
                    name="mm_in_ret")
        y_fwd = s5_direction(ua, *p['s5_fwd'][l], bsz, False)
        ya = s5_direction(ua, *p['s5_bwd'][l], bsz, True,
                          fin=(y_fwd, p['s5_d'][l], p['s5_w_glu'][l], p['s5_b_glu'][l]))
        yb = pool_mixer(ub, p['pool_w'][l], p['pool_scale'][l], seq)
        part = retention_direction(hq, cos, sin, bsz, seq, False)
        yc = retention_direction(hq, cos, sin, bsz, seq, True, part=part)
        x, xb, x8 = mix_ln(p['peer_scales'], x, ya.reshape(seq, bsz * S5_WIDTH), yb, yc, p['w_out'], l,
                           p['ln1_g'][l], p['ln1_b'][l], bsz, seq)
        e1, cnt, e2, r2 = peer_scores(xb, p['peer_w_q'], p['peer_keys'], l)
        x, xb = peer_experts(p['peer_scales'], x8, e1, cnt, e2, r2, p['peer_u'], p['peer_vt'], l,
                             x, p['ln2_g'][l], p['ln2_b'][l])
    return x


def kernel(x_prompt, x_sample, ln_in_g, ln_in_b, w_in, s5_lambda_re, s5_lambda_im, s5_log_step, s5_b_re, s5_b_im, s5_c_re, s5_c_im, s5_d, s5_w_glu, s5_b_glu, pool_w, pool_scale, w_out, ln1_g, ln1_b, peer_w_q, peer_sub_keys, peer_u, peer_v, ln2_g, ln2_b):
    d = D_MODEL

    def s5_dir(direction):
        return [s5_weights(s5_lambda_re[l, direction], s5_lambda_im[l, direction], s5_log_step[l, direction],
                           s5_b_re[l, direction], s5_b_im[l, direction], s5_c_re[l, direction],
                           s5_c_im[l, direction]) for l in range(DEPTH)]

    u_scale = _pow2_scale(jnp.max(jnp.abs(peer_u), axis=(1, 2), keepdims=True))
    x_scale = _pow2_scale(math.sqrt(d) * jnp.max(jnp.abs(ln1_g), axis=1) + jnp.max(jnp.abs(ln1_b), axis=1))
    peer_scales = jnp.zeros((DEPTH, 1, LANES), F32)
    peer_scales = peer_scales.at[:, 0, 0].set(x_scale)
    peer_scales = peer_scales.at[:, 0, 1].set(1.0 / (u_scale[:, 0, 0] * x_scale))

    p = dict(
        ln_in_g=ln_in_g, ln_in_b=ln_in_b,
        w_in=w_in.astype(BF16),
        s5_fwd=s5_dir(0), s5_bwd=s5_dir(1),
        s5_d=s5_d.reshape(DEPTH, 1, S5_WIDTH), s5_w_glu=s5_w_glu.astype(BF16),
        s5_b_glu=s5_b_glu.reshape(DEPTH, 1, S5_WIDTH),
        pool_w=pool_w.astype(BF16), pool_scale=pool_scale.reshape(DEPTH, 1, POOL_WIDTH),
        w_out=w_out.astype(BF16),
        ln1_g=ln1_g.reshape(DEPTH, 1, d), ln1_b=ln1_b.reshape(DEPTH, 1, d),
        peer_w_q=peer_w_q.astype(BF16), peer_keys=peer_sub_keys.astype(BF16),
        peer_u=(peer_u * u_scale).astype(FP8), peer_vt=jnp.swapaxes(peer_v, 1, 2).astype(BF16),
        peer_scales=peer_scales,
        ln2_g=ln2_g.reshape(DEPTH, 1, d), ln2_b=ln2_b.reshape(DEPTH, 1, d),
    )
    outs = []
    for x in (x_prompt, x_sample):
        bsz, seq, _ = x.shape
        outs.append(_trunk(x.reshape(bsz * seq, d), bsz, seq, p).reshape(bsz, seq, d))
    return tuple(outs)
```
